```python
import jax, jax.numpy as jnp
from jax import lax
import numpy as np

D_MODEL = 2048
BATCH = 8
SEQ = 2048
DEPTH = 2
DEC_BATCH = 8
DEC_SEQ = 16
PAST_LEN = 2048

CHUNK = 64
EPS = 1e-6
A_WIDTH = 1024
A_GROUPS = 4
A_GROUP_DIM = A_WIDTH // A_GROUPS
A_CHUNK = 128
B_HEADS = 16
B_Q_RANK = 512
B_KV_RANK = 512
B_NOPE = 128
B_ROPE = 64
B_VDIM = 128
B_QK = B_NOPE + B_ROPE
B_WIDTH = B_HEADS * B_VDIM
ATTN_SCALE = B_QK ** -0.5
ROPE_BASE = 10000.0
Q_BLOCK = 128
C_WIDTH = 1024
C_CONV = 3
N_BRANCH = 3

IN_SIZES = (A_WIDTH, A_WIDTH, A_WIDTH,
            B_Q_RANK, B_KV_RANK, B_ROPE, B_WIDTH,
            C_WIDTH, C_WIDTH, C_WIDTH, C_WIDTH,
            N_BRANCH * D_MODEL)
IN_SPLITS = tuple(sum(IN_SIZES[:i + 1]) for i in range(len(IN_SIZES) - 1))
N_IN = sum(IN_SIZES)

kernel_name = 'hybrid_chunk_streaming_encoder_step'


def rms_norm(x, g):
    xf = x.astype(jnp.float32)
    y = xf * lax.rsqrt(jnp.mean(xf * xf, axis=-1, keepdims=True) + EPS)
    return (y * g.astype(jnp.float32)).astype(x.dtype)


def rope(x, pos):
    half = x.shape[-1] // 2
    inv = ROPE_BASE ** (-jnp.arange(half, dtype=jnp.float32) / half)
    ang = pos[:, None] * inv[None, :]
    ang = ang.reshape(ang.shape[:1] + (1,) * (x.ndim - 3) + (half,))
    c, s = jnp.cos(ang), jnp.sin(ang)
    xf = x.astype(jnp.float32)
    x1, x2 = xf[..., :half], xf[..., half:]
    return jnp.concatenate([x1 * c - x2 * s, x2 * c + x1 * s], axis=-1).astype(x.dtype)


def spatial_gate_prompt(u, vn, ws_m, bias):
    bn, s, _ = vn.shape
    n = s // A_CHUNK
    vg = vn.reshape(bn, n, A_CHUNK, A_GROUPS, A_GROUP_DIM)
    mixed = jnp.einsum('gts,bnsgd->bntgd', ws_m, vg) + bias.T[:, :, None]
    return u * mixed.reshape(bn, s, A_WIDTH)


def spatial_gate_sample(u, vn, ws_m, bias):
    bn, t, _ = vn.shape
    vg = vn.reshape(bn, t, A_GROUPS, A_GROUP_DIM)
    mixed = jnp.einsum('gts,bsgd->btgd', ws_m[:, :t, :t], vg) + bias[:, :t].T[:, :, None]
    return u * mixed.reshape(bn, t, A_WIDTH)


def short_conv(hc, state, w):
    t = hc.shape[1]
    padded = jnp.concatenate([state, hc], axis=1)
    y = sum(w[k] * padded[:, k:k + t] for k in range(C_CONV))
    return y, padded[:, t:]


def mla_queries(b_cq, p, pos):
    cq = rms_norm(b_cq, p['b_q_gain'])
    q = (cq @ p['b_w_qb']).reshape(cq.shape[:2] + (B_HEADS, B_QK))
    q_nope = rms_norm(q[..., :B_NOPE], p['b_qn_gain'])
    q_pe = rope(rms_norm(q[..., B_NOPE:], p['b_qr_gain']), pos)
    return jnp.concatenate([q_nope, q_pe], axis=-1)


def mla_keys_values(ckv_all, kpe_all, p):
    kv = (ckv_all @ p['b_w_kvb']).reshape(ckv_all.shape[:2] + (B_HEADS, B_NOPE + B_VDIM))
    k_nope = rms_norm(kv[..., :B_NOPE], p['b_kn_gain'])
    k_pe = jnp.broadcast_to(kpe_all[:, :, None, :], k_nope.shape[:3] + (B_ROPE,))
    return jnp.concatenate([k_nope, k_pe], axis=-1), kv[..., B_NOPE:]


def attend_prompt(q, k, v):
    bn, s, h, dq = q.shape
    nb = s // Q_BLOCK
    qb = q.reshape(bn, nb, Q_BLOCK, h, dq).swapaxes(0, 1)
    kchunk = jnp.arange(s) // CHUNK

    def one_block(args):
        qblk, i = args
        qchunk = (i * Q_BLOCK + jnp.arange(Q_BLOCK)) // CHUNK
        sc = jnp.einsum('bqhd,bkhd->bhqk', qblk, k, preferred_element_type=jnp.float32) * ATTN_SCALE
        sc = jnp.where((kchunk[None, :] <= qchunk[:, None])[None, None], sc, -jnp.inf)
        pr = jax.nn.softmax(sc, axis=-1).astype(v.dtype)
        return jnp.einsum('bhqk,bkhd->bqhd', pr, v)

    o = lax.map(one_block, (qb, jnp.arange(nb)))
    return o.swapaxes(0, 1).reshape(bn, s, h * B_VDIM)


def attend_sample(q, k, v):
    bn, t = q.shape[:2]
    sc = jnp.einsum('bqhd,bkhd->bhqk', q, k, preferred_element_type=jnp.float32) * ATTN_SCALE
    pr = jax.nn.softmax(sc, axis=-1).astype(v.dtype)
    return jnp.einsum('bhqk,bkhd->bqhd', pr, v).reshape(bn, t, B_HEADS * B_VDIM)


def layer(x, pos, p, conv_state, past_ckv, past_kpe, prompt):
    xn = rms_norm(x, p['norm'])
    proj = xn @ p['w_in']
    (a_u, a_v, a_z, b_cq, b_ckv, b_kpe, b_z,
     c_h, c_b, c_c, c_z, gates) = jnp.split(proj, IN_SPLITS, axis=-1)

    a_vn = rms_norm(a_v.reshape(a_v.shape[:2] + (A_GROUPS, A_GROUP_DIM)),
                    p['a_v_gain'].reshape(A_GROUPS, A_GROUP_DIM)).reshape(a_v.shape)
    ws_m = p['a_ws'] * jnp.tril(jnp.ones((A_CHUNK, A_CHUNK), p['a_ws'].dtype))
    if prompt:
        ya = spatial_gate_prompt(a_u, a_vn, ws_m, p['a_bias'])
    else:
        ya = spatial_gate_sample(a_u, a_vn, ws_m, p['a_bias'])
    ya = ya * jax.nn.silu(a_z)

    ckv_new = rms_norm(b_ckv, p['b_kv_gain'])
    kpe_new = rope(rms_norm(b_kpe, p['b_kr_gain']), pos)
    q = mla_queries(b_cq, p, pos)
    if prompt:
        ckv_all, kpe_all = ckv_new, kpe_new
    else:
        ckv_all = jnp.concatenate([past_ckv, ckv_new], axis=1)
        kpe_all = jnp.concatenate([past_kpe, kpe_new], axis=1)
    k, v = mla_keys_values(ckv_all, kpe_all, p)
    attn = attend_prompt(q, k, v) if prompt else attend_sample(q, k, v)
    yb = attn * jax.nn.silu(b_z)

    conv_y, conv_new = short_conv(c_c * c_h, conv_state, p['c_conv_w'])
    yc = c_b * conv_y * jax.nn.silu(c_z)

    g = jax.nn.sigmoid(gates.astype(jnp.float32)).astype(x.dtype)
    g = g.reshape(g.shape[:-1] + (N_BRANCH, D_MODEL))
    h = (g[..., 0, :] * (ya @ p['w_pa']) + g[..., 1, :] * (yb @ p['w_pb'])
         + g[..., 2, :] * (yc @ p['w_pc']))
    return x + h @ p['w_out'], ckv_new, kpe_new, conv_new, a_vn


def setup_inputs(seed: int = 0) -> dict:
    key = jax.random.key(seed)
    ks = jax.random.split(key, 24)

    def nrm(k, shape, scale):
        return jax.random.normal(k, shape, jnp.float32) * scale

    def gain(k, shape):
        return 1.0 + 0.05 * jax.random.normal(k, shape, jnp.float32)

    return {
        'x_prompt': nrm(ks[0], (BATCH, SEQ, D_MODEL), 1.0),
        'x_sample': nrm(ks[1], (DEC_BATCH, DEC_SEQ, D_MODEL), 1.0),
        'cache_ckv': nrm(ks[2], (DEPTH, DEC_BATCH, PAST_LEN, B_KV_RANK), 1.0),
        'cache_kpe': nrm(ks[3], (DEPTH, DEC_BATCH, PAST_LEN, B_ROPE), 1.0),
        'state_conv': nrm(ks[4], (DEPTH, DEC_BATCH, C_CONV - 1, C_WIDTH), 1.0),
        'norm_gain': gain(ks[5], (DEPTH, D_MODEL)),
        'w_in': nrm(ks[6], (DEPTH, D_MODEL, N_IN), D_MODEL ** -0.5),
        'a_v_gain': gain(ks[7], (DEPTH, A_WIDTH)),
        'a_ws': nrm(ks[8], (DEPTH, A_GROUPS, A_CHUNK, A_CHUNK), A_CHUNK ** -0.5),
        'a_bias': 1.0 + 0.1 * jax.random.normal(ks[9], (DEPTH, A_GROUPS, A_CHUNK), jnp.float32),
        'b_q_gain': gain(ks[10], (DEPTH, B_Q_RANK)),
        'b_w_qb': nrm(ks[11], (DEPTH, B_Q_RANK, B_HEADS * B_QK), B_Q_RANK ** -0.5),
        'b_kv_gain': gain(ks[12], (DEPTH, B_KV_RANK)),
        'b_kr_gain': gain(ks[13], (DEPTH, B_ROPE)),
        'b_w_kvb': nrm(ks[14], (DEPTH, B_KV_RANK, B_HEADS * (B_NOPE + B_VDIM)), B_KV_RANK ** -0.5),
        'b_qn_gain': gain(ks[15], (DEPTH, B_NOPE)),
        'b_qr_gain': gain(ks[16], (DEPTH, B_ROPE)),
        'b_kn_gain': gain(ks[17], (DEPTH, B_NOPE)),
        'c_conv_w': nrm(ks[18], (DEPTH, C_CONV, C_WIDTH), C_CONV ** -0.5),
        'w_pa': nrm(ks[19], (DEPTH, A_WIDTH, D_MODEL), A_WIDTH ** -0.5),
        'w_pb': nrm(ks[20], (DEPTH, B_WIDTH, D_MODEL), B_WIDTH ** -0.5),
        'w_pc': nrm(ks[21], (DEPTH, C_WIDTH, D_MODEL), C_WIDTH ** -0.5),
        'w_out': nrm(ks[22], (DEPTH, D_MODEL, D_MODEL), D_MODEL ** -0.5),
    }


def reference(x_prompt, x_sample, cache_ckv, cache_kpe, state_conv,
              norm_gain, w_in, a_v_gain, a_ws, a_bias,
              b_q_gain, b_w_qb, b_kv_gain, b_kr_gain, b_w_kvb,
              b_qn_gain, b_qr_gain, b_kn_gain, c_conv_w,
              w_pa, w_pb, w_pc, w_out):
    def params(l):
        return {'norm': norm_gain[l], 'w_in': w_in[l], 'a_v_gain': a_v_gain[l],
                'a_ws': a_ws[l], 'a_bias': a_bias[l], 'b_q_gain': b_q_gain[l],
                'b_w_qb': b_w_qb[l], 'b_kv_gain': b_kv_gain[l], 'b_kr_gain': b_kr_gain[l],
                'b_w_kvb': b_w_kvb[l], 'b_qn_gain': b_qn_gain[l], 'b_qr_gain': b_qr_gain[l],
                'b_kn_gain': b_kn_gain[l], 'c_conv_w': c_conv_w[l], 'w_pa': w_pa[l],
                'w_pb': w_pb[l], 'w_pc': w_pc[l], 'w_out': w_out[l]}

    hp = x_prompt
    pos_p = jnp.arange(x_prompt.shape[1], dtype=jnp.float32)
    conv0 = jnp.zeros((x_prompt.shape[0], C_CONV - 1, C_WIDTH), x_prompt.dtype)
    ckv_p, kpe_p, conv_p = [], [], []
    for l in range(DEPTH):
        hp, ckv, kpe, cst, _ = layer(hp, pos_p, params(l), conv0, None, None, True)
        ckv_p.append(ckv)
        kpe_p.append(kpe)
        conv_p.append(cst)

    hs = x_sample
    past_len = cache_ckv.shape[2]
    pos_s = past_len + jnp.arange(x_sample.shape[1], dtype=jnp.float32)
    ckv_s, kpe_s, conv_s, av_s = [], [], [], []
    for l in range(DEPTH):
        hs, ckv, kpe, cst, avn = layer(hs, pos_s, params(l), state_conv[l],
                                       cache_ckv[l], cache_kpe[l], False)
        ckv_s.append(ckv)
        kpe_s.append(kpe)
        conv_s.append(cst)
        av_s.append(avn)

    return (hp, hs,
            jnp.stack(ckv_p), jnp.stack(kpe_p), jnp.stack(conv_p),
            jnp.stack(ckv_s), jnp.stack(kpe_s), jnp.stack(conv_s), jnp.stack(av_s))
```

```python
import functools

import jax
import jax.numpy as jnp
from jax import lax
from jax.experimental import pallas as pl
from jax.experimental.pallas import tpu as pltpu

D_MODEL = 2048
CHUNK = 64
EPS = 1e-6
A_WIDTH = 1024
A_GROUPS = 4
A_GROUP_DIM = A_WIDTH // A_GROUPS
A_CHUNK = 128
B_HEADS = 16
B_Q_RANK = 512
B_KV_RANK = 512
B_NOPE = 128
B_ROPE = 64
B_VDIM = 128
B_QK = B_NOPE + B_ROPE
B_WIDTH = B_HEADS * B_VDIM
ATTN_SCALE = B_QK ** -0.5
ROPE_BASE = 10000.0
C_WIDTH = 1024
C_CONV = 3
N_BRANCH = 3

LANES = 128
HEAD_PAD = 2 * LANES
QK_WIDTH = B_HEADS * HEAD_PAD
B_IN_WIDTH = B_Q_RANK + B_KV_RANK + LANES + B_WIDTH
OFF_CKV = B_Q_RANK
OFF_KPE = B_Q_RANK + B_KV_RANK
OFF_BZ = OFF_KPE + LANES

VMEM_LIMIT = 56 * 1024 * 1024
BF16 = jnp.bfloat16
F32 = jnp.float32


def _dot(a, b):
    return jnp.dot(a, b, preferred_element_type=F32)


def _rms(x, gain, width=None):
    width = x.shape[-1] if width is None else width
    ms = jnp.sum(x * x, axis=-1, keepdims=True) * (1.0 / width)
    return x * lax.rsqrt(ms + EPS) * gain


def _silu(x):
    return x * (1.0 / (1.0 + jnp.exp(-x)))


def _sigmoid(x):
    return 1.0 / (1.0 + jnp.exp(-x))


def _params(sem):
    return pltpu.CompilerParams(dimension_semantics=sem, vmem_limit_bytes=VMEM_LIMIT)


def _resident(shape):
    nd = len(shape)
    return pl.BlockSpec(shape, lambda *_: (0,) * nd, pipeline_mode=pl.Buffered(1))


def _norm_kernel(x_ref, g_ref, o_ref):
    o_ref[...] = _rms(x_ref[...], g_ref[...]).astype(o_ref.dtype)


def _norm(x, gain, tm):
    m = x.shape[0]
    return pl.pallas_call(
        _norm_kernel,
        grid=(m // tm,),
        in_specs=[pl.BlockSpec((tm, D_MODEL), lambda i: (i, 0)), _resident((1, D_MODEL))],
        out_specs=pl.BlockSpec((tm, D_MODEL), lambda i: (i, 0)),
        out_shape=jax.ShapeDtypeStruct((m, D_MODEL), BF16),
        compiler_params=_params(("parallel",)),
        name="norm",
    )(x, gain)


def _branch_a_kernel(xn_ref, w_ref, gain_ref, ws_ref, bias_ref, ya_ref, *vn_ref, tm):
    xn = xn_ref[...]
    row = lax.broadcasted_iota(jnp.int32, (A_CHUNK, A_CHUNK), 0)
    col = lax.broadcasted_iota(jnp.int32, (A_CHUNK, A_CHUNK), 1)
    for g in range(A_GROUPS):
        lo, hi = g * A_GROUP_DIM, (g + 1) * A_GROUP_DIM
        u = _dot(xn, w_ref[:, lo:hi])
        v = _dot(xn, w_ref[:, A_WIDTH + lo:A_WIDTH + hi])
        z = _dot(xn, w_ref[:, 2 * A_WIDTH + lo:2 * A_WIDTH + hi])
        vn = _rms(v, gain_ref[:, lo:hi])
        if vn_ref:
            vn_ref[0][:, lo:hi] = vn
        gate = u * _silu(z)
        ws = jnp.where(col <= row, ws_ref[g], 0.0).astype(BF16)
        bias = bias_ref[:, g:g + 1]
        vnb = vn.astype(BF16)
        for c in range(tm // A_CHUNK):
            r0, r1 = c * A_CHUNK, (c + 1) * A_CHUNK
            mixed = _dot(ws, vnb[r0:r1]) + bias
            ya_ref[r0:r1, lo:hi] = (gate[r0:r1] * mixed).astype(ya_ref.dtype)


def _branch_a(xn, w_a, gain, ws, bias_t, tm, emit_vn):
    m = xn.shape[0]
    out_shape = [jax.ShapeDtypeStruct((m, A_WIDTH), BF16)]
    out_specs = [pl.BlockSpec((tm, A_WIDTH), lambda i: (i, 0))]
    if emit_vn:
        out_shape.append(jax.ShapeDtypeStruct((m, A_WIDTH), F32))
        out_specs.append(pl.BlockSpec((tm, A_WIDTH), lambda i: (i, 0)))
    return pl.pallas_call(
        functools.partial(_branch_a_kernel, tm=tm),
        grid=(m // tm,),
        in_specs=[pl.BlockSpec((tm, D_MODEL), lambda i: (i, 0)),
                  _resident(w_a.shape), _resident(gain.shape),
                  _resident(ws.shape), _resident(bias_t.shape)],
        out_specs=out_specs,
        out_shape=out_shape,
        compiler_params=_params(("parallel",)),
        name="branch_a",
    )(xn, w_a, gain, ws, bias_t)


def _branch_c_kernel(xn_ref, w_ref, cw_ref, state_ref, yc_ref, new_ref, pad_ref, *, seg, carry):
    tm = xn_ref.shape[0]
    xn = xn_ref[...]
    h = _dot(xn, w_ref[:, 0:C_WIDTH])
    b = _dot(xn, w_ref[:, C_WIDTH:2 * C_WIDTH])
    c = _dot(xn, w_ref[:, 2 * C_WIDTH:3 * C_WIDTH])
    z = _dot(xn, w_ref[:, 3 * C_WIDTH:4 * C_WIDTH])
    hc = c * h
    gate = b * _silu(z)
    w0, w1, w2 = cw_ref[0:1, :], cw_ref[1:2, :], cw_ref[2:3, :]
    stride = seg + 8
    if carry:
        first = pl.program_id(0) % carry == 0

        @pl.when(first)
        def _():
            pad_ref[6:8, :] = state_ref[0]
    for s in range(tm // seg):
        base = s * stride
        if not carry:
            pad_ref[base + 6:base + 8, :] = state_ref[s]
        pad_ref[base + 8:base + 8 + seg, :] = hc[s * seg:(s + 1) * seg]
        y = (w2 * hc[s * seg:(s + 1) * seg]
             + w1 * pad_ref[base + 7:base + 7 + seg, :]
             + w0 * pad_ref[base + 6:base + 6 + seg, :])
        yc_ref[s * seg:(s + 1) * seg, :] = (gate[s * seg:(s + 1) * seg] * y).astype(yc_ref.dtype)
        new_ref[s] = hc[(s + 1) * seg - 2:(s + 1) * seg]
    if carry:
        pad_ref[6:8, :] = hc[tm - 2:tm]


def _branch_c(xn, w_c, conv_w, state, tm, seg, tiles_per_stream):
    m = xn.shape[0]
    n_streams = state.shape[0]
    nseg = tm // seg
    if tiles_per_stream:
        st_map = lambda i: (i // tiles_per_stream, 0, 0)
    else:
        st_map = lambda i: (i, 0, 0)
    return pl.pallas_call(
        functools.partial(_branch_c_kernel, seg=seg, carry=tiles_per_stream),
        grid=(m // tm,),
        in_specs=[pl.BlockSpec((tm, D_MODEL), lambda i: (i, 0)),
                  _resident(w_c.shape), _resident(conv_w.shape),
                  pl.BlockSpec((nseg, C_CONV - 1, C_WIDTH), st_map)],
        out_specs=[pl.BlockSpec((tm, C_WIDTH), lambda i: (i, 0)),
                   pl.BlockSpec((nseg, C_CONV - 1, C_WIDTH), st_map)],
        out_shape=[jax.ShapeDtypeStruct((m, C_WIDTH), BF16),
                   jax.ShapeDtypeStruct((n_streams, C_CONV - 1, C_WIDTH), F32)],
        scratch_shapes=[pltpu.VMEM((nseg * (seg + 8), C_WIDTH), F32)],
        compiler_params=_params(("arbitrary",)),
        name="branch_c",
    )(xn, w_c, conv_w, state)


def _rope_padded(x, cos_ref, sin_lo_ref, sin_hi_ref):
    return (x * cos_ref[...]
            + pltpu.roll(x, LANES - B_ROPE // 2, 1) * sin_lo_ref[...]
            + pltpu.roll(x, B_ROPE // 2, 1) * sin_hi_ref[...])


def _expand_kv(ckv_n, kpe_r, wkvb_ref, kng_ref, k_ref, v_ref):
    kv = _dot(ckv_n.astype(BF16), wkvb_ref[...])
    v_ref[...] = kv[:, B_HEADS * B_NOPE:].astype(v_ref.dtype)
    kpe_b = kpe_r.astype(k_ref.dtype)
    for hd in range(B_HEADS):
        kn = _rms(kv[:, hd * B_NOPE:(hd + 1) * B_NOPE], kng_ref[...])
        k_ref[:, hd * HEAD_PAD:hd * HEAD_PAD + LANES] = kn.astype(k_ref.dtype)
        k_ref[:, hd * HEAD_PAD + LANES:(hd + 1) * HEAD_PAD] = kpe_b


def _branch_b_kernel(xn_ref, wb_ref, wqb_ref, wkvb_ref, qg_ref, kvg_ref, krg_ref, qng_ref, qrg_ref,
                     kng_ref, cos_ref, sin_lo_ref, sin_hi_ref,
                     q_ref, k_ref, v_ref, ckv_ref, kpe_ref, zb_ref):
    xn = xn_ref[...]
    cq = _rms(_dot(xn, wb_ref[:, 0:B_Q_RANK]), qg_ref[...])
    q = _dot(cq.astype(BF16), wqb_ref[...])
    for hd in range(B_HEADS):
        qa = _rms(q[:, hd * HEAD_PAD:hd * HEAD_PAD + LANES], qng_ref[...])
        qb = _rms(q[:, hd * HEAD_PAD + LANES:(hd + 1) * HEAD_PAD], qrg_ref[...], B_ROPE)
        q_ref[:, hd * HEAD_PAD:hd * HEAD_PAD + LANES] = qa.astype(q_ref.dtype)
        q_ref[:, hd * HEAD_PAD + LANES:(hd + 1) * HEAD_PAD] = _rope_padded(
            qb, cos_ref, sin_lo_ref, sin_hi_ref).astype(q_ref.dtype)
    ckv_n = _rms(_dot(xn, wb_ref[:, OFF_CKV:OFF_CKV + B_KV_RANK]), kvg_ref[...])
    ckv_ref[...] = ckv_n
    kpe = _rms(_dot(xn, wb_ref[:, OFF_KPE:OFF_KPE + LANES]), krg_ref[...], B_ROPE)
    kpe_r = _rope_padded(kpe, cos_ref, sin_lo_ref, sin_hi_ref)
    kpe_ref[...] = kpe_r[:, 0:B_ROPE]
    _expand_kv(ckv_n, kpe_r, wkvb_ref, kng_ref, k_ref, v_ref)
    zb_ref[...] = _silu(_dot(xn, wb_ref[:, OFF_BZ:OFF_BZ + B_WIDTH]))


def _branch_b(xn, w_b, wqb, wkvb, gains, tables, tm, table_tiles):
    m = xn.shape[0]
    row = lambda w: pl.BlockSpec((tm, w), lambda i: (i, 0))
    tab = pl.BlockSpec((tm, LANES), lambda i: (i % table_tiles, 0))
    return pl.pallas_call(
        _branch_b_kernel,
        grid=(m // tm,),
        in_specs=[row(D_MODEL), _resident(w_b.shape), _resident(wqb.shape), _resident(wkvb.shape)]
                 + [_resident(g.shape) for g in gains] + [tab, tab, tab],
        out_specs=[row(QK_WIDTH), row(QK_WIDTH), row(B_WIDTH), row(B_KV_RANK), row(B_ROPE), row(B_WIDTH)],
        out_shape=[jax.ShapeDtypeStruct((m, QK_WIDTH), BF16),
                   jax.ShapeDtypeStruct((m, QK_WIDTH), BF16),
                   jax.ShapeDtypeStruct((m, B_WIDTH), BF16),
                   jax.ShapeDtypeStruct((m, B_KV_RANK), F32),
                   jax.ShapeDtypeStruct((m, B_ROPE), F32),
                   jax.ShapeDtypeStruct((m, B_WIDTH), F32)],
        compiler_params=_params(("parallel",)),
        name="branch_b",
    )(xn, w_b, wqb, wkvb, *gains, *tables)


def _expand_cache_kernel(ckv_ref, kpe_ref, wkvb_ref, kng_ref, k_ref, v_ref):
    _expand_kv(ckv_ref[...], kpe_ref[...], wkvb_ref, kng_ref, k_ref, v_ref)


def _expand_cache(ckv, kpe, wkvb, kn_gain, tm):
    m = ckv.shape[0]
    row = lambda w: pl.BlockSpec((tm, w), lambda i: (i, 0))
    return pl.pallas_call(
        _expand_cache_kernel,
        grid=(m // tm,),
        in_specs=[row(B_KV_RANK), row(LANES), _resident(wkvb.shape), _resident(kn_gain.shape)],
        out_specs=[row(QK_WIDTH), row(B_WIDTH)],
        out_shape=[jax.ShapeDtypeStruct((m, QK_WIDTH), BF16), jax.ShapeDtypeStruct((m, B_WIDTH), BF16)],
        compiler_params=_params(("parallel",)),
        name="expand_cache",
    )(ckv, kpe, wkvb, kn_gain)


def _scores(q, k):
    return lax.dot_general(q, k, (((1,), (1,)), ((), ())), preferred_element_type=F32) * ATTN_SCALE


def _attn_prompt_kernel(q_ref, k_ref, v_ref, zb_ref, o_ref, *, tq):
    s_len = q_ref.shape[0]
    for qi in range(s_len // tq):
        r0, kv_len = qi * tq, (qi + 1) * tq
        sc = _scores(q_ref[r0:r0 + tq, :], k_ref[0:kv_len, :])
        limit = (lax.broadcasted_iota(jnp.int32, (tq, 1), 0) // CHUNK + 1) * CHUNK + r0
        sc = jnp.where(lax.broadcasted_iota(jnp.int32, (tq, kv_len), 1) < limit, sc, -jnp.inf)
        p = jnp.exp(sc - jnp.max(sc, axis=-1, keepdims=True))
        denom = jnp.sum(p, axis=-1, keepdims=True)
        o = _dot(p.astype(BF16), v_ref[0:kv_len, :]) / denom
        o_ref[r0:r0 + tq, :] = (o * zb_ref[r0:r0 + tq, :]).astype(o_ref.dtype)


def _attn_prompt(q, k, v, zb, tq):
    bn, s_len, _ = q.shape
    blk = lambda w: pl.BlockSpec((None, s_len, w), lambda b, h: (b, 0, h))
    return pl.pallas_call(
        functools.partial(_attn_prompt_kernel, tq=tq),
        grid=(bn, B_HEADS),
        in_specs=[blk(HEAD_PAD), blk(HEAD_PAD), blk(B_VDIM), blk(B_VDIM)],
        out_specs=blk(B_VDIM),
        out_shape=jax.ShapeDtypeStruct((bn, s_len, B_WIDTH), BF16),
        compiler_params=_params(("parallel", "parallel")),
        name="attn_prompt",
    )(q, k, v, zb)


def _attn_sample_kernel(q_ref, kp_ref, kn_ref, vp_ref, vn_ref, zb_ref, o_ref):
    q = q_ref[...]
    sp = _scores(q, kp_ref[...])
    sn = _scores(q, kn_ref[...])
    mx = jnp.maximum(jnp.max(sp, axis=-1, keepdims=True), jnp.max(sn, axis=-1, keepdims=True))
    pp = jnp.exp(sp - mx)
    pn = jnp.exp(sn - mx)
    denom = jnp.sum(pp, axis=-1, keepdims=True) + jnp.sum(pn, axis=-1, keepdims=True)
    o = (_dot(pp.astype(BF16), vp_ref[...]) + _dot(pn.astype(BF16), vn_ref[...])) / denom
    o_ref[...] = (o * zb_ref[...]).astype(o_ref.dtype)


def _attn_sample(q, k_past, k_new, v_past, v_new, zb):
    bn, t, _ = q.shape
    past = k_past.shape[1]
    blk = lambda n, w: pl.BlockSpec((None, n, w), lambda b, h: (b, 0, h))
    return pl.pallas_call(
        _attn_sample_kernel,
        grid=(bn, B_HEADS),
        in_specs=[blk(t, HEAD_PAD), blk(past, HEAD_PAD), blk(t, HEAD_PAD),
                  blk(past, B_VDIM), blk(t, B_VDIM), blk(t, B_VDIM)],
        out_specs=blk(t, B_VDIM),
        out_shape=jax.ShapeDtypeStruct((bn, t, B_WIDTH), BF16),
        compiler_params=_params(("parallel", "parallel")),
        name="attn_sample",
    )(q, k_past, k_new, v_past, v_new, zb)


def _merge_kernel(xn_ref, ya_ref, yb_ref, yc_ref, wg0_ref, wg1_ref, wg2_ref,
                  wpa_ref, wpb_ref, wpc_ref, h_ref):
    xn = xn_ref[...]
    h = (_sigmoid(_dot(xn, wg0_ref[...])) * _dot(ya_ref[...], wpa_ref[...])
         + _sigmoid(_dot(xn, wg1_ref[...])) * _dot(yb_ref[...], wpb_ref[...])
         + _sigmoid(_dot(xn, wg2_ref[...])) * _dot(yc_ref[...], wpc_ref[...]))
    h_ref[...] = h.astype(h_ref.dtype)


def _merge(xn, ya, yb, yc, w_g, w_pa, w_pb, w_pc, tm, tn):
    m = xn.shape[0]
    nj = D_MODEL // tn
    row = lambda w: pl.BlockSpec((tm, w), lambda i, j: (i, 0))
    colw = lambda k, off: pl.BlockSpec((k, tn), lambda i, j, off=off: (0, j + off))
    return pl.pallas_call(
        _merge_kernel,
        grid=(m // tm, nj),
        in_specs=[row(D_MODEL), row(A_WIDTH), row(B_WIDTH), row(C_WIDTH),
                  colw(D_MODEL, 0), colw(D_MODEL, nj), colw(D_MODEL, 2 * nj),
                  colw(A_WIDTH, 0), colw(B_WIDTH, 0), colw(C_WIDTH, 0)],
        out_specs=pl.BlockSpec((tm, tn), lambda i, j: (i, j)),
        out_shape=jax.ShapeDtypeStruct((m, D_MODEL), BF16),
        compiler_params=_params(("parallel", "parallel")),
        name="merge",
    )(xn, ya, yb, yc, w_g, w_g, w_g, w_pa, w_pb, w_pc)


def _out_kernel(x_ref, h_ref, w_ref, o_ref):
    o_ref[...] = x_ref[...] + _dot(h_ref[...], w_ref[...])


def _out_proj(x, h, w_out, tm):
    m = x.shape[0]
    row = pl.BlockSpec((tm, D_MODEL), lambda i: (i, 0))
    return pl.pallas_call(
        _out_kernel,
        grid=(m // tm,),
        in_specs=[row, row, _resident(w_out.shape)],
        out_specs=row,
        out_shape=jax.ShapeDtypeStruct((m, D_MODEL), F32),
        compiler_params=_params(("parallel",)),
        name="out_proj",
    )(x, h, w_out)


def _rope_tables(pos):
    half = B_ROPE // 2
    inv = ROPE_BASE ** (-jnp.arange(half, dtype=F32) / half)
    ang = pos[:, None] * inv[None, :]
    c, s = jnp.cos(ang), jnp.sin(ang)
    z = jnp.zeros_like(c)
    return (jnp.concatenate([c, c, z, z], axis=1),
            jnp.concatenate([-s, z, z, z], axis=1),
            jnp.concatenate([z, s, z, z], axis=1))


def _layer_weights(l, norm_gain, w_in, a_v_gain, a_ws, a_bias, b_q_gain, b_w_qb, b_kv_gain, b_kr_gain,
                   b_w_kvb, b_qn_gain, b_qr_gain, b_kn_gain, c_conv_w, w_pa, w_pb, w_pc, w_out):
    w = w_in[l]
    o_b = 3 * A_WIDTH
    o_bz = o_b + B_Q_RANK + B_KV_RANK + B_ROPE
    o_c = o_bz + B_WIDTH
    o_g = o_c + 4 * C_WIDTH
    w_b = jnp.concatenate([w[:, o_b:o_bz], jnp.zeros((D_MODEL, LANES - B_ROPE), w.dtype),
                           w[:, o_bz:o_c]], axis=1)
    wqb = jnp.pad(b_w_qb[l].reshape(B_Q_RANK, B_HEADS, B_QK),
                  ((0, 0), (0, 0), (0, HEAD_PAD - B_QK))).reshape(B_Q_RANK, QK_WIDTH)
    wkv = b_w_kvb[l].reshape(B_KV_RANK, B_HEADS, B_NOPE + B_VDIM)
    wkvb = jnp.concatenate([wkv[:, :, :B_NOPE].reshape(B_KV_RANK, B_HEADS * B_NOPE),
                            wkv[:, :, B_NOPE:].reshape(B_KV_RANK, B_WIDTH)], axis=1)
    pad_rope = lambda g: jnp.pad(g[l], (0, LANES - B_ROPE))[None, :]
    return dict(
        norm=norm_gain[l][None, :],
        w_a=w[:, :o_b].astype(BF16), w_b=w_b.astype(BF16), w_c=w[:, o_c:o_g].astype(BF16),
        w_g=w[:, o_g:].astype(BF16),
        a_v_gain=a_v_gain[l][None, :], a_ws=a_ws[l], a_bias=a_bias[l],
        wqb=wqb.astype(BF16), wkvb=wkvb.astype(BF16),
        b_gains=(b_q_gain[l][None, :], b_kv_gain[l][None, :], pad_rope(b_kr_gain),
                 b_qn_gain[l][None, :], pad_rope(b_qr_gain), b_kn_gain[l][None, :]),
        conv_w=c_conv_w[l],
        w_pa=w_pa[l].astype(BF16), w_pb=w_pb[l].astype(BF16), w_pc=w_pc[l].astype(BF16),
        w_out=w_out[l].astype(BF16))


def _finish_layer(x, xn, ya, yb, yc, p, tm_merge, tm_out):
    h = _merge(xn, ya, yb, yc, p['w_g'], p['w_pa'], p['w_pb'], p['w_pc'], tm_merge, 256)
    return _out_proj(x, h, p['w_out'], tm_out)


def _prompt_layer(x, p, tables, bn, s_len):
    tm = 512
    xn = _norm(x, p['norm'], tm)
    (ya,) = _branch_a(xn, p['w_a'], p['a_v_gain'], p['a_ws'], p['a_bias'].T, tm, False)
    zero_state = jnp.zeros((bn, C_CONV - 1, C_WIDTH), F32)
    yc, conv_new = _branch_c(xn, p['w_c'], p['conv_w'], zero_state, tm, tm, s_len // tm)
    tm_b = 256
    q, k, v, ckv, kpe, zb = _branch_b(xn, p['w_b'], p['wqb'], p['wkvb'], p['b_gains'], tables,
                                      tm_b, s_len // tm_b)
    r3 = lambda a: a.reshape(bn, s_len, a.shape[-1])
    yb = _attn_prompt(r3(q), r3(k), r3(v), r3(zb), 256).reshape(bn * s_len, B_WIDTH)
    out = _finish_layer(x, xn, ya, yb, yc, p, 1024, tm)
    return out, r3(ckv), r3(kpe), conv_new


def _sample_layer(x, p, tables, bn, t, conv_state, past_ckv, past_kpe):
    m = bn * t
    past = past_ckv.shape[1]
    xn = _norm(x, p['norm'], m)
    ws_blk = jnp.einsum('ab,gts->gatbs', jnp.eye(bn, dtype=F32), p['a_ws'][:, :t, :t]).reshape(
        A_GROUPS, m, m)
    bias_t = jnp.tile(p['a_bias'][:, :t], (1, bn)).T
    ya, a_vn = _branch_a(xn, p['w_a'], p['a_v_gain'], ws_blk, bias_t, m, True)
    yc, conv_new = _branch_c(xn, p['w_c'], p['conv_w'], conv_state, m, t, 0)
    q, k, v, ckv, kpe, zb = _branch_b(xn, p['w_b'], p['wqb'], p['wkvb'], p['b_gains'], tables, m, 1)
    kpe_pad = jnp.pad(past_kpe.reshape(bn * past, B_ROPE), ((0, 0), (0, LANES - B_ROPE)))
    k_past, v_past = _expand_cache(past_ckv.reshape(bn * past, B_KV_RANK), kpe_pad,
                                   p['wkvb'], p['b_gains'][5], 512)
    r3 = lambda a, n: a.reshape(bn, n, a.shape[-1])
    yb = _attn_sample(r3(q, t), r3(k_past, past), r3(k, t), r3(v_past, past), r3(v, t),
                      r3(zb, t)).reshape(m, B_WIDTH)
    out = _finish_layer(x, xn, ya, yb, yc, p, m, m)
    return out, r3(ckv, t), r3(kpe, t), conv_new, r3(a_vn, t)


def kernel(x_prompt, x_sample, cache_ckv, cache_kpe, state_conv, norm_gain, w_in, a_v_gain, a_ws, a_bias,
           b_q_gain, b_w_qb, b_kv_gain, b_kr_gain, b_w_kvb, b_qn_gain, b_qr_gain, b_kn_gain, c_conv_w,
           w_pa, w_pb, w_pc, w_out):
    depth = w_in.shape[0]
    bn, s_len, _ = x_prompt.shape
    dbn, t, _ = x_sample.shape
    past = cache_ckv.shape[2]
    assert A_CHUNK % t == 0 and dbn * t == A_CHUNK and past % A_CHUNK == 0

    weights = [_layer_weights(l, norm_gain, w_in, a_v_gain, a_ws, a_bias, b_q_gain, b_w_qb, b_kv_gain,
                              b_kr_gain, b_w_kvb, b_qn_gain, b_qr_gain, b_kn_gain, c_conv_w,
                              w_pa, w_pb, w_pc, w_out) for l in range(depth)]
    tab_p = _rope_tables(jnp.arange(s_len, dtype=F32))
    tab_s = tuple(jnp.tile(tb, (dbn, 1)) for tb in _rope_tables(past + jnp.arange(t, dtype=F32)))

    hp = x_prompt.reshape(bn * s_len, D_MODEL)
    ckv_p, kpe_p, conv_p = [], [], []
    for l in range(depth):
        hp, ckv, kpe, cst = _prompt_layer(hp, weights[l], tab_p, bn, s_len)
        ckv_p.append(ckv)
        kpe_p.append(kpe)
        conv_p.append(cst)

    hs = x_sample.reshape(dbn * t, D_MODEL)
    ckv_s, kpe_s, conv_s, av_s = [], [], [], []
    for l in range(depth):
        hs, ckv, kpe, cst, avn = _sample_layer(hs, weights[l], tab_s, dbn, t, state_conv[l],
                                               cache_ckv[l], cache_kpe[l])
        ckv_s.append(ckv)
        kpe_s.append(kpe)
        conv_s.append(cst)
        av_s.append(avn)

    return (hp.reshape(bn, s_len, D_MODEL), hs.reshape(dbn, t, D_MODEL),
            jnp.stack(ckv_p), jnp.stack(kpe_p), jnp.stack(conv_p),
            jnp.stack(ckv_s), jnp.stack(kpe_s), jnp.stack(conv_s), jnp.stack(av_s))
```

```python
import functools
import math

import jax
import jax.numpy as jnp
from jax import lax
from jax.experimental import pallas as pl
from jax.experimental.pallas import tpu as pltpu

D_MODEL = 2048
CHUNK = 64
EPS = 1e-6
A_WIDTH = 1024
A_GROUPS = 4
A_GROUP_DIM = A_WIDTH // A_GROUPS
A_CHUNK = 128
B_HEADS = 16
B_Q_RANK = 512
B_KV_RANK = 512
B_NOPE = 128
B_ROPE = 64
B_VDIM = 128
B_QK = B_NOPE + B_ROPE
B_WIDTH = B_HEADS * B_VDIM
ATTN_SCALE = B_QK ** -0.5
ROPE_BASE = 10000.0
C_WIDTH = 1024
C_CONV = 3
N_BRANCH = 3

LANES = 128
HEAD_PAD = 2 * LANES
QK_WIDTH = B_HEADS * HEAD_PAD

R_C = 0
R_BZ = R_C + 4 * C_WIDTH
R_A = R_BZ + B_WIDTH
R_CQKV = R_A + 3 * A_WIDTH
R_KPE = R_CQKV + B_Q_RANK + B_KV_RANK
R_G = R_KPE + 2 * LANES
R_TOTAL = R_G + N_BRANCH * D_MODEL

VMEM_LIMIT = 56 * 1024 * 1024
BF16 = jnp.bfloat16
F32 = jnp.float32


def _dot(a, b):
    return jnp.dot(a, b, preferred_element_type=F32)


def _rms(x, gain, width=None):
    width = x.shape[-1] if width is None else width
    ms = jnp.sum(x * x, axis=-1, keepdims=True) * (1.0 / width)
    return x * lax.rsqrt(ms + EPS) * gain


def _silu(x):
    return x * (1.0 / (1.0 + jnp.exp(-x)))


def _sigmoid(x):
    return 1.0 / (1.0 + jnp.exp(-x))


def _params(sem):
    return pltpu.CompilerParams(dimension_semantics=sem, vmem_limit_bytes=VMEM_LIMIT)


def _layer_block(l, shape, blk=0):
    nd = len(shape)
    return pl.BlockSpec((None,) + tuple(shape), lambda *_: (l,) + (0,) * (nd - 1) + (blk,),
                        pipeline_mode=pl.Buffered(1))


def _w_in_block(l, width, offset):
    return _layer_block(l, (D_MODEL, width), offset // width)


def _rows(tm, width):
    return pl.BlockSpec((tm, width), lambda i: (i, 0))


def _norm_kernel(x_ref, g_ref, o_ref):
    o_ref[...] = _rms(x_ref[...], g_ref[...]).astype(o_ref.dtype)


def _norm(x, gain, l, tm):
    m = x.shape[0]
    return pl.pallas_call(
        _norm_kernel,
        grid=(m // tm,),
        in_specs=[_rows(tm, D_MODEL), _layer_block(l, (1, D_MODEL))],
        out_specs=_rows(tm, D_MODEL),
        out_shape=jax.ShapeDtypeStruct((m, D_MODEL), BF16),
        compiler_params=_params(("parallel",)),
        name="norm",
    )(x, gain)


def _branch_a_kernel(xn_ref, w_ref, gain_ref, ws_ref, bias_ref, ya_ref, *vn_ref, tm):
    xn = xn_ref[...]
    row = lax.broadcasted_iota(jnp.int32, (A_CHUNK, A_CHUNK), 0)
    col = lax.broadcasted_iota(jnp.int32, (A_CHUNK, A_CHUNK), 1)
    for g in range(A_GROUPS):
        lo, hi = g * A_GROUP_DIM, (g + 1) * A_GROUP_DIM
        u = _dot(xn, w_ref[:, lo:hi])
        v = _dot(xn, w_ref[:, A_WIDTH + lo:A_WIDTH + hi])
        z = _dot(xn, w_ref[:, 2 * A_WIDTH + lo:2 * A_WIDTH + hi])
        vn = _rms(v, gain_ref[:, lo:hi])
        if vn_ref:
            vn_ref[0][:, lo:hi] = vn
        gate = u * _silu(z)
        ws = jnp.where(col <= row, ws_ref[g], 0.0).astype(BF16)
        bias = bias_ref[:, g:g + 1]
        vnb = vn.astype(BF16)
        for c in range(tm // A_CHUNK):
            r0, r1 = c * A_CHUNK, (c + 1) * A_CHUNK
            mixed = _dot(ws, vnb[r0:r1]) + bias
            ya_ref[r0:r1, lo:hi] = (gate[r0:r1] * mixed).astype(ya_ref.dtype)


def _branch_a(xn, p, l, ws, bias_t, tm, emit_vn):
    m = xn.shape[0]
    out_shape = [jax.ShapeDtypeStruct((m, A_WIDTH), BF16)]
    out_specs = [_rows(tm, A_WIDTH)]
    if emit_vn:
        out_shape.append(jax.ShapeDtypeStruct((m, A_WIDTH), F32))
        out_specs.append(_rows(tm, A_WIDTH))
    return pl.pallas_call(
        functools.partial(_branch_a_kernel, tm=tm),
        grid=(m // tm,),
        in_specs=[_rows(tm, D_MODEL), _w_in_block(l, 3 * A_WIDTH, R_A),
                  _layer_block(l, (1, A_WIDTH)), _layer_block(l, ws.shape[1:]),
                  _layer_block(l, bias_t.shape[1:])],
        out_specs=out_specs,
        out_shape=out_shape,
        compiler_params=_params(("parallel",)),
        name="branch_a",
    )(xn, p['w_in'], p['a_v_gain'], ws, bias_t)


def _branch_c_kernel(xn_ref, w_ref, cw_ref, state_ref, yc_ref, new_ref, pad_ref, *, seg, carry):
    tm = xn_ref.shape[0]
    xn = xn_ref[...]
    h = _dot(xn, w_ref[:, 0:C_WIDTH])
    b = _dot(xn, w_ref[:, C_WIDTH:2 * C_WIDTH])
    c = _dot(xn, w_ref[:, 2 * C_WIDTH:3 * C_WIDTH])
    z = _dot(xn, w_ref[:, 3 * C_WIDTH:4 * C_WIDTH])
    hc = c * h
    gate = b * _silu(z)
    w0, w1, w2 = cw_ref[0:1, :], cw_ref[1:2, :], cw_ref[2:3, :]
    stride = seg + 8
    if carry:
        first = pl.program_id(0) % carry == 0

        @pl.when(first)
        def _():
            pad_ref[6:8, :] = state_ref[0]
    for s in range(tm // seg):
        base = s * stride
        if not carry:
            pad_ref[base + 6:base + 8, :] = state_ref[s]
        pad_ref[base + 8:base + 8 + seg, :] = hc[s * seg:(s + 1) * seg]
        y = (w2 * hc[s * seg:(s + 1) * seg]
             + w1 * pad_ref[base + 7:base + 7 + seg, :]
             + w0 * pad_ref[base + 6:base + 6 + seg, :])
        yc_ref[s * seg:(s + 1) * seg, :] = (gate[s * seg:(s + 1) * seg] * y).astype(yc_ref.dtype)
        new_ref[s] = hc[(s + 1) * seg - 2:(s + 1) * seg]
    if carry:
        pad_ref[6:8, :] = hc[tm - 2:tm]


def _branch_c(xn, p, l, state, state_layer, tm, seg, tiles_per_stream):
    m = xn.shape[0]
    n_streams = state.shape[1]
    nseg = tm // seg
    blk = (nseg, C_CONV - 1, C_WIDTH)
    if tiles_per_stream:
        st_in = pl.BlockSpec((None,) + blk, lambda i: (state_layer, i // tiles_per_stream, 0, 0))
        st_out = pl.BlockSpec(blk, lambda i: (i // tiles_per_stream, 0, 0))
    else:
        st_in = pl.BlockSpec((None,) + blk, lambda i: (state_layer, i, 0, 0))
        st_out = pl.BlockSpec(blk, lambda i: (i, 0, 0))
    return pl.pallas_call(
        functools.partial(_branch_c_kernel, seg=seg, carry=tiles_per_stream),
        grid=(m // tm,),
        in_specs=[_rows(tm, D_MODEL), _w_in_block(l, 4 * C_WIDTH, R_C),
                  _layer_block(l, (C_CONV, C_WIDTH)), st_in],
        out_specs=[_rows(tm, C_WIDTH), st_out],
        out_shape=[jax.ShapeDtypeStruct((m, C_WIDTH), BF16),
                   jax.ShapeDtypeStruct((n_streams, C_CONV - 1, C_WIDTH), F32)],
        scratch_shapes=[pltpu.VMEM((nseg * (seg + 8), C_WIDTH), F32)],
        compiler_params=_params(("arbitrary",)),
        name="branch_c",
    )(xn, p['w_in'], p['conv_w'], state)


def _rope_padded(x, cos_ref, sin_lo_ref, sin_hi_ref):
    return (x * cos_ref[...]
            + pltpu.roll(x, LANES - B_ROPE // 2, 1) * sin_lo_ref[...]
            + pltpu.roll(x, B_ROPE // 2, 1) * sin_hi_ref[...])


def _expand_kv(ckv_n, kpe, wkvb_ref, kng_ref, k_ref, v_ref):
    rows = ckv_n.shape[0]
    kv = _dot(ckv_n.astype(BF16), wkvb_ref[...])
    v_ref[...] = kv[:, B_HEADS * B_NOPE:].astype(v_ref.dtype)
    kpe_b = kpe.astype(k_ref.dtype)
    zeros = jnp.zeros((rows, LANES - B_ROPE), k_ref.dtype)
    for hd in range(B_HEADS):
        c0 = hd * HEAD_PAD
        kn = _rms(kv[:, hd * B_NOPE:(hd + 1) * B_NOPE], kng_ref[...])
        k_ref[:, c0:c0 + LANES] = kn.astype(k_ref.dtype)
        k_ref[:, c0 + LANES:c0 + LANES + B_ROPE] = kpe_b
        k_ref[:, c0 + LANES + B_ROPE:c0 + HEAD_PAD] = zeros


def _branch_b_kernel(xn_ref, wcqkv_ref, wkpe_ref, wbz_ref, wqb_ref, wkvb_ref, qg_ref, kvg_ref, krg_ref,
                     qng_ref, qrg_ref, kng_ref, cos_ref, sin_lo_ref, sin_hi_ref, *rest):
    q_ref, k_ref, v_ref, zb_ref, ckv_ref, kpe_ref = rest[-6:]
    xn = xn_ref[...]
    cq = _rms(_dot(xn, wcqkv_ref[:, 0:B_Q_RANK]), qg_ref[...])
    q = _dot(cq.astype(BF16), wqb_ref[...])
    for hd in range(B_HEADS):
        c0 = hd * HEAD_PAD
        qa = _rms(q[:, c0:c0 + LANES], qng_ref[...])
        qb = _rms(q[:, c0 + LANES:c0 + HEAD_PAD], qrg_ref[...], B_ROPE)
        q_ref[:, c0:c0 + LANES] = qa.astype(q_ref.dtype)
        q_ref[:, c0 + LANES:c0 + HEAD_PAD] = _rope_padded(
            qb, cos_ref, sin_lo_ref, sin_hi_ref).astype(q_ref.dtype)
    ckv_n = _rms(_dot(xn, wcqkv_ref[:, B_Q_RANK:B_Q_RANK + B_KV_RANK]), kvg_ref[...])
    ckv_ref[...] = ckv_n
    kpe = _rope_padded(_rms(_dot(xn, wkpe_ref[...]), krg_ref[...], B_ROPE),
                       cos_ref, sin_lo_ref, sin_hi_ref)[:, 0:B_ROPE]
    kpe_ref[...] = kpe
    _expand_kv(ckv_n, kpe, wkvb_ref, kng_ref, k_ref, v_ref)
    zb_ref[...] = _silu(_dot(xn, wbz_ref[...]))


def _branch_b(xn, p, l, tables, tm, table_tiles, latents):
    m = xn.shape[0]
    depth = p['w_in'].shape[0]
    tab = pl.BlockSpec((tm, LANES), lambda i: (i % table_tiles, 0))
    gains = [p[n] for n in ('b_q_gain', 'b_kv_gain', 'b_kr_gain', 'b_qn_gain', 'b_qr_gain', 'b_kn_gain')]
    acc = [] if latents is None else list(latents)
    n_in = 15
    return pl.pallas_call(
        _branch_b_kernel,
        grid=(m // tm,),
        in_specs=[_rows(tm, D_MODEL),
                  _w_in_block(l, B_Q_RANK + B_KV_RANK, R_CQKV), _w_in_block(l, LANES, R_KPE),
                  _w_in_block(l, B_WIDTH, R_BZ),
                  _layer_block(l, (B_Q_RANK, QK_WIDTH)), _layer_block(l, (B_KV_RANK, 2 * B_WIDTH))]
                 + [_layer_block(l, g.shape[1:]) for g in gains] + [tab, tab, tab]
                 + [pl.BlockSpec(memory_space=pl.ANY)] * len(acc),
        out_specs=[_rows(tm, QK_WIDTH), _rows(tm, QK_WIDTH), _rows(tm, B_WIDTH), _rows(tm, B_WIDTH),
                   pl.BlockSpec((None, tm, B_KV_RANK), lambda i: (l, i, 0)),
                   pl.BlockSpec((None, tm, B_ROPE), lambda i: (l, i, 0))],
        out_shape=[jax.ShapeDtypeStruct((m, QK_WIDTH), BF16),
                   jax.ShapeDtypeStruct((m, QK_WIDTH), BF16),
                   jax.ShapeDtypeStruct((m, B_WIDTH), BF16),
                   jax.ShapeDtypeStruct((m, B_WIDTH), F32),
                   jax.ShapeDtypeStruct((depth, m, B_KV_RANK), F32),
                   jax.ShapeDtypeStruct((depth, m, B_ROPE), F32)],
        input_output_aliases={n_in + j: 4 + j for j in range(len(acc))},
        compiler_params=_params(("parallel",)),
        name="branch_b",
    )(xn, p['w_in'], p['w_in'], p['w_in'], p['wqb'], p['wkvb'], *gains, *tables, *acc)


def _expand_cache_kernel(ckv_ref, kpe_ref, wkvb_ref, kng_ref, k_ref, v_ref):
    _expand_kv(ckv_ref[...], kpe_ref[...], wkvb_ref, kng_ref, k_ref, v_ref)


def _expand_cache(ckv, kpe, p, l, tm):
    m = ckv.shape[1]
    lrows = lambda w: pl.BlockSpec((None, tm, w), lambda i: (l, i, 0))
    return pl.pallas_call(
        _expand_cache_kernel,
        grid=(m // tm,),
        in_specs=[lrows(B_KV_RANK), lrows(B_ROPE), _layer_block(l, (B_KV_RANK, 2 * B_WIDTH)),
                  _layer_block(l, (1, B_NOPE))],
        out_specs=[_rows(tm, QK_WIDTH), _rows(tm, B_WIDTH)],
        out_shape=[jax.ShapeDtypeStruct((m, QK_WIDTH), BF16), jax.ShapeDtypeStruct((m, B_WIDTH), BF16)],
        compiler_params=_params(("parallel",)),
        name="expand_cache",
    )(ckv, kpe, p['wkvb'], p['b_kn_gain'])


def _qk(q, k):
    return lax.dot_general(q, k, (((1,), (1,)), ((), ())), preferred_element_type=F32)


EXP2_SCALE = ATTN_SCALE * math.log2(math.e)


def _attn_prompt_kernel(q_ref, k_ref, v_ref, zb_ref, o_ref, vext_ref, *, tq, heads):
    s_len = q_ref.shape[0]
    rowc = lax.broadcasted_iota(jnp.int32, (tq, tq), 0) // CHUNK
    colc = lax.broadcasted_iota(jnp.int32, (tq, tq), 1) // CHUNK
    visible = colc <= rowc
    for hh in range(heads):
        qs = slice(hh * HEAD_PAD, (hh + 1) * HEAD_PAD)
        vs = slice(hh * B_VDIM, (hh + 1) * B_VDIM)
        vext = vext_ref.at[hh]
        vext[:, 0:B_VDIM] = v_ref[:, vs]
        vext[:, B_VDIM:] = jnp.ones((s_len, LANES), vext_ref.dtype)
        for qi in range(s_len // tq):
            r0 = qi * tq
            q_t = q_ref[r0:r0 + tq, qs]
            s_diag = jnp.where(visible, _qk(q_t, k_ref[r0:r0 + tq, qs]), -jnp.inf)
            mx = jnp.max(s_diag, axis=-1, keepdims=True)
            if qi:
                s_prev = _qk(q_t, k_ref[0:r0, qs])
                mx = jnp.maximum(mx, jnp.max(s_prev, axis=-1, keepdims=True))
            mc = mx * EXP2_SCALE
            p_diag = jnp.exp2(s_diag * EXP2_SCALE - mc).astype(BF16)
            acc = _dot(p_diag, vext[r0:r0 + tq, :])
            if qi:
                p_prev = jnp.exp2(s_prev * EXP2_SCALE - mc).astype(BF16)
                acc = acc + _dot(p_prev, vext[0:r0, :])
            o = acc[:, 0:B_VDIM] / acc[:, B_VDIM:]
            o_ref[r0:r0 + tq, vs] = (o * zb_ref[r0:r0 + tq, vs]).astype(o_ref.dtype)


def _attn_prompt(q, k, v, zb, tq, heads):
    bn, s_len, _ = q.shape
    blk = lambda w: pl.BlockSpec((None, s_len, heads * w), lambda b, h: (b, 0, h))
    return pl.pallas_call(
        functools.partial(_attn_prompt_kernel, tq=tq, heads=heads),
        grid=(bn, B_HEADS // heads),
        in_specs=[blk(HEAD_PAD), blk(HEAD_PAD), blk(B_VDIM), blk(B_VDIM)],
        out_specs=blk(B_VDIM),
        out_shape=jax.ShapeDtypeStruct((bn, s_len, B_WIDTH), BF16),
        scratch_shapes=[pltpu.VMEM((heads, s_len, B_VDIM + LANES), BF16)],
        compiler_params=_params(("parallel", "parallel")),
        name="attn_prompt",
    )(q, k, v, zb)


def _attn_sample_kernel(q_ref, kp_ref, kn_ref, vp_ref, vn_ref, zb_ref, o_ref):
    q = q_ref[...]
    sp = _qk(q, kp_ref[...]) * ATTN_SCALE
    sn = _qk(q, kn_ref[...]) * ATTN_SCALE
    mx = jnp.maximum(jnp.max(sp, axis=-1, keepdims=True), jnp.max(sn, axis=-1, keepdims=True))
    pp = jnp.exp(sp - mx)
    pn = jnp.exp(sn - mx)
    denom = jnp.sum(pp, axis=-1, keepdims=True) + jnp.sum(pn, axis=-1, keepdims=True)
    o = (_dot(pp.astype(BF16), vp_ref[...]) + _dot(pn.astype(BF16), vn_ref[...])) / denom
    o_ref[...] = (o * zb_ref[...]).astype(o_ref.dtype)


def _attn_sample(q, k_past, k_new, v_past, v_new, zb):
    bn, t, _ = q.shape
    past = k_past.shape[1]
    blk = lambda n, w: pl.BlockSpec((None, n, w), lambda b, h: (b, 0, h))
    return pl.pallas_call(
        _attn_sample_kernel,
        grid=(bn, B_HEADS),
        in_specs=[blk(t, HEAD_PAD), blk(past, HEAD_PAD), blk(t, HEAD_PAD),
                  blk(past, B_VDIM), blk(t, B_VDIM), blk(t, B_VDIM)],
        out_specs=blk(t, B_VDIM),
        out_shape=jax.ShapeDtypeStruct((bn, t, B_WIDTH), BF16),
        compiler_params=_params(("parallel", "parallel")),
        name="attn_sample",
    )(q, k_past, k_new, v_past, v_new, zb)


def _merge_kernel(xn_ref, ya_ref, yb_ref, yc_ref, wg0_ref, wg1_ref, wg2_ref,
                  wpa_ref, wpb_ref, wpc_ref, h_ref):
    xn = xn_ref[...]
    h = (_sigmoid(_dot(xn, wg0_ref[...])) * _dot(ya_ref[...], wpa_ref[...])
         + _sigmoid(_dot(xn, wg1_ref[...])) * _dot(yb_ref[...], wpb_ref[...])
         + _sigmoid(_dot(xn, wg2_ref[...])) * _dot(yc_ref[...], wpc_ref[...]))
    h_ref[...] = h.astype(h_ref.dtype)


def _merge(xn, ya, yb, yc, p, l, tm, tn):
    m = xn.shape[0]
    nj = D_MODEL // tn
    row = lambda w: pl.BlockSpec((tm, w), lambda i, j: (i, 0))
    colw = lambda k, off: pl.BlockSpec((None, k, tn), lambda i, j: (l, 0, j + off))
    g0 = R_G // tn
    return pl.pallas_call(
        _merge_kernel,
        grid=(m // tm, nj),
        in_specs=[row(D_MODEL), row(A_WIDTH), row(B_WIDTH), row(C_WIDTH),
                  colw(D_MODEL, g0), colw(D_MODEL, g0 + nj), colw(D_MODEL, g0 + 2 * nj),
                  colw(A_WIDTH, 0), colw(B_WIDTH, 0), colw(C_WIDTH, 0)],
        out_specs=pl.BlockSpec((tm, tn), lambda i, j: (i, j)),
        out_shape=jax.ShapeDtypeStruct((m, D_MODEL), BF16),
        compiler_params=_params(("parallel", "parallel")),
        name="merge",
    )(xn, ya, yb, yc, p['w_in'], p['w_in'], p['w_in'], p['w_pa'], p['w_pb'], p['w_pc'])


def _out_kernel(x_ref, h_ref, w_ref, o_ref):
    o_ref[...] = x_ref[...] + _dot(h_ref[...], w_ref[...])


def _out_proj(x, h, p, l, tm):
    m = x.shape[0]
    return pl.pallas_call(
        _out_kernel,
        grid=(m // tm,),
        in_specs=[_rows(tm, D_MODEL), _rows(tm, D_MODEL), _layer_block(l, (D_MODEL, D_MODEL))],
        out_specs=_rows(tm, D_MODEL),
        out_shape=jax.ShapeDtypeStruct((m, D_MODEL), F32),
        compiler_params=_params(("parallel",)),
        name="out_proj",
    )(x, h, p['w_out'])


def _rope_tables(pos):
    half = B_ROPE // 2
    inv = ROPE_BASE ** (-jnp.arange(half, dtype=F32) / half)
    ang = pos[:, None] * inv[None, :]
    c, s = jnp.cos(ang), jnp.sin(ang)
    z = jnp.zeros_like(c)
    return (jnp.concatenate([c, c, z, z], axis=1),
            jnp.concatenate([-s, z, z, z], axis=1),
            jnp.concatenate([z, s, z, z], axis=1))


def _prepare_params(norm_gain, w_in, a_v_gain, a_ws, a_bias, b_q_gain, b_w_qb, b_kv_gain, b_kr_gain,
                    b_w_kvb, b_qn_gain, b_qr_gain, b_kn_gain, c_conv_w, w_pa, w_pb, w_pc, w_out):
    depth = w_in.shape[0]
    o_b = 3 * A_WIDTH
    o_kpe = o_b + B_Q_RANK + B_KV_RANK
    o_bz = o_kpe + B_ROPE
    o_c = o_bz + B_WIDTH
    o_g = o_c + 4 * C_WIDTH
    w_r = jnp.concatenate(
        [w_in[:, :, o_c:o_g], w_in[:, :, o_bz:o_c], w_in[:, :, :o_b], w_in[:, :, o_b:o_kpe],
         w_in[:, :, o_kpe:o_bz], jnp.zeros((depth, D_MODEL, 2 * LANES - B_ROPE), w_in.dtype),
         w_in[:, :, o_g:]], axis=2).astype(BF16)
    assert w_r.shape[2] == R_TOTAL
    wqb = jnp.pad(b_w_qb.reshape(depth, B_Q_RANK, B_HEADS, B_QK),
                  ((0, 0), (0, 0), (0, 0), (0, HEAD_PAD - B_QK))).reshape(depth, B_Q_RANK, QK_WIDTH)
    wkv = b_w_kvb.reshape(depth, B_KV_RANK, B_HEADS, B_NOPE + B_VDIM)
    wkvb = jnp.concatenate([wkv[..., :B_NOPE].reshape(depth, B_KV_RANK, B_HEADS * B_NOPE),
                            wkv[..., B_NOPE:].reshape(depth, B_KV_RANK, B_WIDTH)], axis=2)
    row = lambda g: g[:, None, :]
    pad_rope = lambda g: jnp.pad(g, ((0, 0), (0, LANES - B_ROPE)))[:, None, :]
    return dict(
        norm=row(norm_gain), w_in=w_r, a_v_gain=row(a_v_gain), a_ws=a_ws, a_bias=a_bias,
        wqb=wqb.astype(BF16), wkvb=wkvb.astype(BF16),
        b_q_gain=row(b_q_gain), b_kv_gain=row(b_kv_gain), b_kr_gain=pad_rope(b_kr_gain),
        b_qn_gain=row(b_qn_gain), b_qr_gain=pad_rope(b_qr_gain), b_kn_gain=row(b_kn_gain),
        conv_w=c_conv_w,
        w_pa=w_pa.astype(BF16), w_pb=w_pb.astype(BF16), w_pc=w_pc.astype(BF16),
        w_out=w_out.astype(BF16))


def _finish_layer(x, xn, ya, yb, yc, p, l, tm_merge, tm_out):
    h = _merge(xn, ya, yb, yc, p, l, tm_merge, 256)
    return _out_proj(x, h, p, l, tm_out)


def _prompt_layer(x, p, l, tables, bn, s_len, latents, zero_state):
    tm = 512
    xn = _norm(x, p['norm'], l, tm)
    (ya,) = _branch_a(xn, p, l, p['a_ws'], jnp.swapaxes(p['a_bias'], 1, 2), tm, False)
    yc, conv_new = _branch_c(xn, p, l, zero_state, 0, tm, tm, s_len // tm)
    tm_b = 256
    q, k, v, zb, ckv, kpe = _branch_b(xn, p, l, tables, tm_b, s_len // tm_b, latents)
    r3 = lambda a: a.reshape(bn, s_len, a.shape[-1])
    yb = _attn_prompt(r3(q), r3(k), r3(v), r3(zb), 256, 2).reshape(bn * s_len, B_WIDTH)
    out = _finish_layer(x, xn, ya, yb, yc, p, l, 1024, tm)
    return out, (ckv, kpe), conv_new


def _sample_layer(x, p, l, tables, bn, t, ws_blk, bias_t, state_conv, cache_ckv, cache_kpe, latents):
    m = bn * t
    past = cache_ckv.shape[1] // bn
    xn = _norm(x, p['norm'], l, m)
    ya, a_vn = _branch_a(xn, p, l, ws_blk, bias_t, m, True)
    yc, conv_new = _branch_c(xn, p, l, state_conv, l, m, t, 0)
    q, k, v, zb, ckv, kpe = _branch_b(xn, p, l, tables, m, 1, latents)
    k_past, v_past = _expand_cache(cache_ckv, cache_kpe, p, l, 512)
    r3 = lambda a, n: a.reshape(bn, n, a.shape[-1])
    yb = _attn_sample(r3(q, t), r3(k_past, past), r3(k, t), r3(v_past, past), r3(v, t),
                      r3(zb, t)).reshape(m, B_WIDTH)
    out = _finish_layer(x, xn, ya, yb, yc, p, l, m, m)
    return out, (ckv, kpe), conv_new, a_vn


def kernel(x_prompt, x_sample, cache_ckv, cache_kpe, state_conv, norm_gain, w_in, a_v_gain, a_ws, a_bias,
           b_q_gain, b_w_qb, b_kv_gain, b_kr_gain, b_w_kvb, b_qn_gain, b_qr_gain, b_kn_gain, c_conv_w,
           w_pa, w_pb, w_pc, w_out):
    depth = w_in.shape[0]
    bn, s_len, _ = x_prompt.shape
    dbn, t, _ = x_sample.shape
    past = cache_ckv.shape[2]
    assert A_CHUNK % t == 0 and dbn * t == A_CHUNK and past % A_CHUNK == 0

    p = _prepare_params(norm_gain, w_in, a_v_gain, a_ws, a_bias, b_q_gain, b_w_qb, b_kv_gain, b_kr_gain,
                        b_w_kvb, b_qn_gain, b_qr_gain, b_kn_gain, c_conv_w, w_pa, w_pb, w_pc, w_out)
    tab_p = _rope_tables(jnp.arange(s_len, dtype=F32))
    tab_s = tuple(jnp.tile(tb, (dbn, 1)) for tb in _rope_tables(past + jnp.arange(t, dtype=F32)))

    hp = x_prompt.reshape(bn * s_len, D_MODEL)
    zero_state = jnp.zeros((1, bn, C_CONV - 1, C_WIDTH), F32)
    lat_p, conv_p = None, []
    for l in range(depth):
        hp, lat_p, cst = _prompt_layer(hp, p, l, tab_p, bn, s_len, lat_p, zero_state)
        conv_p.append(cst)

    m_s = dbn * t
    ws_blk = jnp.einsum('ab,lgts->lgatbs', jnp.eye(dbn, dtype=F32), a_ws[:, :, :t, :t]).reshape(
        depth, A_GROUPS, m_s, m_s)
    bias_s = jnp.swapaxes(jnp.tile(a_bias[:, :, :t], (1, 1, dbn)), 1, 2)
    ckv_c = cache_ckv.reshape(depth, dbn * past, B_KV_RANK)
    kpe_c = cache_kpe.reshape(depth, dbn * past, B_ROPE)
    hs = x_sample.reshape(m_s, D_MODEL)
    lat_s, conv_s, av_s = None, [], []
    for l in range(depth):
        hs, lat_s, cst, avn = _sample_layer(hs, p, l, tab_s, dbn, t, ws_blk, bias_s, state_conv,
                                            ckv_c, kpe_c, lat_s)
        conv_s.append(cst)
        av_s.append(avn.reshape(dbn, t, A_WIDTH))

    return (hp.reshape(bn, s_len, D_MODEL), hs.reshape(dbn, t, D_MODEL),
            lat_p[0].reshape(depth, bn, s_len, B_KV_RANK), lat_p[1].reshape(depth, bn, s_len, B_ROPE),
            jnp.stack(conv_p),
            lat_s[0].reshape(depth, dbn, t, B_KV_RANK), lat_s[1].reshape(depth, dbn, t, B_ROPE),
            jnp.stack(conv_s), jnp.stack(av_s))
```

```python
import functools
import math

import jax
import jax.numpy as jnp
from jax import lax
from jax.experimental import pallas as pl
from jax.experimental.pallas import tpu as pltpu

D_MODEL = 2048
CHUNK = 64
EPS = 1e-6
A_WIDTH = 1024
A_GROUPS = 4
A_GROUP_DIM = A_WIDTH // A_GROUPS
A_CHUNK = 128
B_HEADS = 16
B_Q_RANK = 512
B_KV_RANK = 512
B_NOPE = 128
B_ROPE = 64
B_VDIM = 128
B_QK = B_NOPE + B_ROPE
B_WIDTH = B_HEADS * B_VDIM
ATTN_SCALE = B_QK ** -0.5
ROPE_BASE = 10000.0
C_WIDTH = 1024
C_CONV = 3
N_BRANCH = 3

LANES = 128
HEAD_PAD = 2 * LANES
QK_WIDTH = B_HEADS * HEAD_PAD

R_C = 0
R_BZ = R_C + 4 * C_WIDTH
R_A = R_BZ + B_WIDTH
R_CQKV = R_A + 3 * A_WIDTH
R_KPE = R_CQKV + B_Q_RANK + B_KV_RANK
R_G = R_KPE + 2 * LANES
R_TOTAL = R_G + N_BRANCH * D_MODEL

VMEM_LIMIT = 56 * 1024 * 1024
BF16 = jnp.bfloat16
F32 = jnp.float32


def _dot(a, b):
    return jnp.dot(a, b, preferred_element_type=F32)


def _rms(x, gain, width=None):
    width = x.shape[-1] if width is None else width
    ms = jnp.sum(x * x, axis=-1, keepdims=True) * (1.0 / width)
    return x * lax.rsqrt(ms + EPS) * gain


def _silu(x):
    return x * (1.0 / (1.0 + jnp.exp(-x)))


def _sigmoid(x):
    return 1.0 / (1.0 + jnp.exp(-x))


def _params(sem):
    return pltpu.CompilerParams(dimension_semantics=sem, vmem_limit_bytes=VMEM_LIMIT)


def _layer_block(l, shape, blk=0):
    nd = len(shape)
    return pl.BlockSpec((None,) + tuple(shape), lambda *_: (l,) + (0,) * (nd - 1) + (blk,),
                        pipeline_mode=pl.Buffered(1))


def _w_in_block(l, width, offset):
    return _layer_block(l, (D_MODEL, width), offset // width)


def _rows(tm, width):
    return pl.BlockSpec((tm, width), lambda i: (i, 0))


def _norm_kernel(x_ref, g_ref, o_ref):
    o_ref[...] = _rms(x_ref[...], g_ref[...]).astype(o_ref.dtype)


def _norm(x, gain, l, tm):
    m = x.shape[0]
    return pl.pallas_call(
        _norm_kernel,
        grid=(m // tm,),
        in_specs=[_rows(tm, D_MODEL), _layer_block(l, (1, D_MODEL))],
        out_specs=_rows(tm, D_MODEL),
        out_shape=jax.ShapeDtypeStruct((m, D_MODEL), BF16),
        compiler_params=_params(("parallel",)),
        name="norm",
    )(x, gain)


def _branch_a_kernel(xn_ref, w_ref, gain_ref, ws_ref, bias_ref, ya_ref, *vn_ref, tm):
    xn = xn_ref[...]
    row = lax.broadcasted_iota(jnp.int32, (A_CHUNK, A_CHUNK), 0)
    col = lax.broadcasted_iota(jnp.int32, (A_CHUNK, A_CHUNK), 1)
    for g in range(A_GROUPS):
        lo, hi = g * A_GROUP_DIM, (g + 1) * A_GROUP_DIM
        u = _dot(xn, w_ref[:, lo:hi])
        v = _dot(xn, w_ref[:, A_WIDTH + lo:A_WIDTH + hi])
        z = _dot(xn, w_ref[:, 2 * A_WIDTH + lo:2 * A_WIDTH + hi])
        vn = _rms(v, gain_ref[:, lo:hi])
        if vn_ref:
            vn_ref[0][:, lo:hi] = vn
        gate = u * _silu(z)
        ws = jnp.where(col <= row, ws_ref[g], 0.0).astype(BF16)
        bias = bias_ref[:, g:g + 1]
        vnb = vn.astype(BF16)
        for c in range(tm // A_CHUNK):
            r0, r1 = c * A_CHUNK, (c + 1) * A_CHUNK
            mixed = _dot(ws, vnb[r0:r1]) + bias
            ya_ref[r0:r1, lo:hi] = (gate[r0:r1] * mixed).astype(ya_ref.dtype)


def _branch_a(xn, p, l, ws, bias_t, tm, emit_vn):
    m = xn.shape[0]
    out_shape = [jax.ShapeDtypeStruct((m, A_WIDTH), BF16)]
    out_specs = [_rows(tm, A_WIDTH)]
    if emit_vn:
        out_shape.append(jax.ShapeDtypeStruct((m, A_WIDTH), F32))
        out_specs.append(_rows(tm, A_WIDTH))
    return pl.pallas_call(
        functools.partial(_branch_a_kernel, tm=tm),
        grid=(m // tm,),
        in_specs=[_rows(tm, D_MODEL), _w_in_block(l, 3 * A_WIDTH, R_A),
                  _layer_block(l, (1, A_WIDTH)), _layer_block(l, ws.shape[1:]),
                  _layer_block(l, bias_t.shape[1:])],
        out_specs=out_specs,
        out_shape=out_shape,
        compiler_params=_params(("parallel",)),
        name="branch_a",
    )(xn, p['w_in'], p['a_v_gain'], ws, bias_t)


def _branch_c_kernel(xn_ref, w_ref, cw_ref, state_ref, yc_ref, new_ref, pad_ref, *, seg, carry):
    tm = xn_ref.shape[0]
    xn = xn_ref[...]
    h = _dot(xn, w_ref[:, 0:C_WIDTH])
    b = _dot(xn, w_ref[:, C_WIDTH:2 * C_WIDTH])
    c = _dot(xn, w_ref[:, 2 * C_WIDTH:3 * C_WIDTH])
    z = _dot(xn, w_ref[:, 3 * C_WIDTH:4 * C_WIDTH])
    hc = c * h
    gate = b * _silu(z)
    w0, w1, w2 = cw_ref[0:1, :], cw_ref[1:2, :], cw_ref[2:3, :]
    stride = seg + 8
    if carry:
        first = pl.program_id(0) % carry == 0

        @pl.when(first)
        def _():
            pad_ref[6:8, :] = state_ref[0]
    for s in range(tm // seg):
        base = s * stride
        if not carry:
            pad_ref[base + 6:base + 8, :] = state_ref[s]
        pad_ref[base + 8:base + 8 + seg, :] = hc[s * seg:(s + 1) * seg]
        y = (w2 * hc[s * seg:(s + 1) * seg]
             + w1 * pad_ref[base + 7:base + 7 + seg, :]
             + w0 * pad_ref[base + 6:base + 6 + seg, :])
        yc_ref[s * seg:(s + 1) * seg, :] = (gate[s * seg:(s + 1) * seg] * y).astype(yc_ref.dtype)
        new_ref[s] = hc[(s + 1) * seg - 2:(s + 1) * seg]
    if carry:
        pad_ref[6:8, :] = hc[tm - 2:tm]


def _branch_c(xn, p, l, state, state_layer, tm, seg, tiles_per_stream):
    m = xn.shape[0]
    n_streams = state.shape[1]
    nseg = tm // seg
    blk = (nseg, C_CONV - 1, C_WIDTH)
    if tiles_per_stream:
        st_in = pl.BlockSpec((None,) + blk, lambda i: (state_layer, i // tiles_per_stream, 0, 0))
        st_out = pl.BlockSpec(blk, lambda i: (i // tiles_per_stream, 0, 0))
    else:
        st_in = pl.BlockSpec((None,) + blk, lambda i: (state_layer, i, 0, 0))
        st_out = pl.BlockSpec(blk, lambda i: (i, 0, 0))
    return pl.pallas_call(
        functools.partial(_branch_c_kernel, seg=seg, carry=tiles_per_stream),
        grid=(m // tm,),
        in_specs=[_rows(tm, D_MODEL), _w_in_block(l, 4 * C_WIDTH, R_C),
                  _layer_block(l, (C_CONV, C_WIDTH)), st_in],
        out_specs=[_rows(tm, C_WIDTH), st_out],
        out_shape=[jax.ShapeDtypeStruct((m, C_WIDTH), BF16),
                   jax.ShapeDtypeStruct((n_streams, C_CONV - 1, C_WIDTH), F32)],
        scratch_shapes=[pltpu.VMEM((nseg * (seg + 8), C_WIDTH), F32)],
        compiler_params=_params(("arbitrary",)),
        name="branch_c",
    )(xn, p['w_in'], p['conv_w'], state)


def _rope_padded(x, cos_ref, sin_lo_ref, sin_hi_ref):
    return (x * cos_ref[...]
            + pltpu.roll(x, LANES - B_ROPE // 2, 1) * sin_lo_ref[...]
            + pltpu.roll(x, B_ROPE // 2, 1) * sin_hi_ref[...])


def _expand_kv(ckv_n, kpe_pad, wkvb_ref, kng_ref, k_ref, v_ref):
    kv = _dot(ckv_n.astype(BF16), wkvb_ref[...])
    v_ref[...] = kv[:, B_HEADS * B_NOPE:].astype(v_ref.dtype)
    kpe_b = kpe_pad.astype(k_ref.dtype)
    kn = [_rms(kv[:, hd * B_NOPE:(hd + 1) * B_NOPE], kng_ref[...]) for hd in range(B_HEADS)]
    for hd in range(B_HEADS):
        k_ref[:, hd * HEAD_PAD:hd * HEAD_PAD + LANES] = kn[hd].astype(k_ref.dtype)
        k_ref[:, hd * HEAD_PAD + LANES:(hd + 1) * HEAD_PAD] = kpe_b


def _branch_b_kernel(xn_ref, wcqkv_ref, wkpe_ref, wbz_ref, wqb_ref, wkvb_ref, qg_ref, kvg_ref, krg_ref,
                     qng_ref, qrg_ref, kng_ref, cos_ref, sin_lo_ref, sin_hi_ref, *rest):
    q_ref, k_ref, v_ref, zb_ref, ckv_ref, kpe_ref = rest[-6:]
    xn = xn_ref[...]
    cq = _rms(_dot(xn, wcqkv_ref[:, 0:B_Q_RANK]), qg_ref[...])
    q = _dot(cq.astype(BF16), wqb_ref[...])
    heads = range(B_HEADS)
    qa = [_rms(q[:, hd * HEAD_PAD:hd * HEAD_PAD + LANES], qng_ref[...]) for hd in heads]
    qb = [_rms(q[:, hd * HEAD_PAD + LANES:(hd + 1) * HEAD_PAD], qrg_ref[...], B_ROPE) for hd in heads]
    qb = [_rope_padded(qb[hd], cos_ref, sin_lo_ref, sin_hi_ref) for hd in heads]
    for hd in heads:
        q_ref[:, hd * HEAD_PAD:hd * HEAD_PAD + LANES] = qa[hd].astype(q_ref.dtype)
        q_ref[:, hd * HEAD_PAD + LANES:(hd + 1) * HEAD_PAD] = qb[hd].astype(q_ref.dtype)
    ckv_n = _rms(_dot(xn, wcqkv_ref[:, B_Q_RANK:B_Q_RANK + B_KV_RANK]), kvg_ref[...])
    ckv_ref[...] = ckv_n
    kpe_pad = _rope_padded(_rms(_dot(xn, wkpe_ref[...]), krg_ref[...], B_ROPE),
                           cos_ref, sin_lo_ref, sin_hi_ref)
    kpe_ref[...] = kpe_pad[:, 0:B_ROPE]
    _expand_kv(ckv_n, kpe_pad, wkvb_ref, kng_ref, k_ref, v_ref)
    zb_ref[...] = _silu(_dot(xn, wbz_ref[...]))


def _branch_b(xn, p, l, tables, tm, table_tiles, latents):
    m = xn.shape[0]
    depth = p['w_in'].shape[0]
    tab = pl.BlockSpec((tm, LANES), lambda i: (i % table_tiles, 0))
    gains = [p[n] for n in ('b_q_gain', 'b_kv_gain', 'b_kr_gain', 'b_qn_gain', 'b_qr_gain', 'b_kn_gain')]
    acc = [] if latents is None else list(latents)
    n_in = 15
    return pl.pallas_call(
        _branch_b_kernel,
        grid=(m // tm,),
        in_specs=[_rows(tm, D_MODEL),
                  _w_in_block(l, B_Q_RANK + B_KV_RANK, R_CQKV), _w_in_block(l, LANES, R_KPE),
                  _w_in_block(l, B_WIDTH, R_BZ),
                  _layer_block(l, (B_Q_RANK, QK_WIDTH)), _layer_block(l, (B_KV_RANK, 2 * B_WIDTH))]
                 + [_layer_block(l, g.shape[1:]) for g in gains] + [tab, tab, tab]
                 + [pl.BlockSpec(memory_space=pl.ANY)] * len(acc),
        out_specs=[_rows(tm, QK_WIDTH), _rows(tm, QK_WIDTH), _rows(tm, B_WIDTH), _rows(tm, B_WIDTH),
                   pl.BlockSpec((None, tm, B_KV_RANK), lambda i: (l, i, 0)),
                   pl.BlockSpec((None, tm, B_ROPE), lambda i: (l, i, 0))],
        out_shape=[jax.ShapeDtypeStruct((m, QK_WIDTH), BF16),
                   jax.ShapeDtypeStruct((m, QK_WIDTH), BF16),
                   jax.ShapeDtypeStruct((m, B_WIDTH), BF16),
                   jax.ShapeDtypeStruct((m, B_WIDTH), F32),
                   jax.ShapeDtypeStruct((depth, m, B_KV_RANK), F32),
                   jax.ShapeDtypeStruct((depth, m, B_ROPE), F32)],
        input_output_aliases={n_in + j: 4 + j for j in range(len(acc))},
        compiler_params=_params(("parallel",)),
        name="branch_b",
    )(xn, p['w_in'], p['w_in'], p['w_in'], p['wqb'], p['wkvb'], *gains, *tables, *acc)


def _expand_cache_kernel(ckv_ref, kpe_ref, wkvb_ref, kng_ref, k_ref, v_ref, pad_ref):
    rows = kpe_ref.shape[0]
    pad_ref[:, 0:B_ROPE] = kpe_ref[...]
    pad_ref[:, B_ROPE:] = jnp.zeros((rows, LANES - B_ROPE), pad_ref.dtype)
    _expand_kv(ckv_ref[...], pad_ref[...], wkvb_ref, kng_ref, k_ref, v_ref)


def _expand_cache(ckv, kpe, p, l, tm):
    m = ckv.shape[1]
    lrows = lambda w: pl.BlockSpec((None, tm, w), lambda i: (l, i, 0))
    return pl.pallas_call(
        _expand_cache_kernel,
        grid=(m // tm,),
        in_specs=[lrows(B_KV_RANK), lrows(B_ROPE), _layer_block(l, (B_KV_RANK, 2 * B_WIDTH)),
                  _layer_block(l, (1, B_NOPE))],
        out_specs=[_rows(tm, QK_WIDTH), _rows(tm, B_WIDTH)],
        out_shape=[jax.ShapeDtypeStruct((m, QK_WIDTH), BF16), jax.ShapeDtypeStruct((m, B_WIDTH), BF16)],
        scratch_shapes=[pltpu.VMEM((tm, LANES), F32)],
        compiler_params=_params(("parallel",)),
        name="expand_cache",
    )(ckv, kpe, p['wkvb'], p['b_kn_gain'])


def _qk(q, k):
    return lax.dot_general(q, k, (((1,), (1,)), ((), ())), preferred_element_type=F32)


EXP2_SCALE = ATTN_SCALE * math.log2(math.e)


def _attn_prompt_kernel(q_ref, k_ref, v_ref, zb_ref, o_ref, vext_ref, *, tq, heads):
    s_len = q_ref.shape[0]
    rowc = lax.broadcasted_iota(jnp.int32, (tq, tq), 0) // CHUNK
    colc = lax.broadcasted_iota(jnp.int32, (tq, tq), 1) // CHUNK
    visible = colc <= rowc
    for hh in range(heads):
        vext_ref[hh, :, 0:B_VDIM] = v_ref[:, hh * B_VDIM:(hh + 1) * B_VDIM]
        vext_ref[hh, :, B_VDIM:] = jnp.ones((s_len, LANES), vext_ref.dtype)
    hs = range(heads)
    qs = [slice(hh * HEAD_PAD, (hh + 1) * HEAD_PAD) for hh in hs]
    vs = [slice(hh * B_VDIM, (hh + 1) * B_VDIM) for hh in hs]

    def scores(qi):
        r0 = qi * tq
        q_t = [q_ref[r0:r0 + tq, qs[hh]] for hh in hs]
        s_diag = [jnp.where(visible, _qk(q_t[hh], k_ref[r0:r0 + tq, qs[hh]]), -jnp.inf) for hh in hs]
        mx = [jnp.max(s_diag[hh], axis=-1, keepdims=True) for hh in hs]
        s_prev = None
        if qi:
            s_prev = [_qk(q_t[hh], k_ref[0:r0, qs[hh]]) for hh in hs]
            mx = [jnp.maximum(mx[hh], jnp.max(s_prev[hh], axis=-1, keepdims=True)) for hh in hs]
        return s_diag, s_prev, mx

    def weighted_values(qi, s_diag, s_prev, mx):
        r0 = qi * tq
        mc = [mx[hh] * EXP2_SCALE for hh in hs]
        p_diag = [jnp.exp2(s_diag[hh] * EXP2_SCALE - mc[hh]).astype(BF16) for hh in hs]
        acc = [_dot(p_diag[hh], vext_ref[hh, r0:r0 + tq, :]) for hh in hs]
        if qi:
            p_prev = [jnp.exp2(s_prev[hh] * EXP2_SCALE - mc[hh]).astype(BF16) for hh in hs]
            acc = [acc[hh] + _dot(p_prev[hh], vext_ref[hh, 0:r0, :]) for hh in hs]
        for hh in hs:
            o = acc[hh][:, 0:B_VDIM] / acc[hh][:, B_VDIM:]
            o_ref[r0:r0 + tq, vs[hh]] = (o * zb_ref[r0:r0 + tq, vs[hh]]).astype(o_ref.dtype)

    for qi in range(s_len // tq):
        weighted_values(qi, *scores(qi))


def _attn_prompt(q, k, v, zb, tq, heads):
    bn, s_len, _ = q.shape
    blk = lambda w: pl.BlockSpec((None, s_len, heads * w), lambda b, h: (b, 0, h))
    return pl.pallas_call(
        functools.partial(_attn_prompt_kernel, tq=tq, heads=heads),
        grid=(bn, B_HEADS // heads),
        in_specs=[blk(HEAD_PAD), blk(HEAD_PAD), blk(B_VDIM), blk(B_VDIM)],
        out_specs=blk(B_VDIM),
        out_shape=jax.ShapeDtypeStruct((bn, s_len, B_WIDTH), BF16),
        scratch_shapes=[pltpu.VMEM((heads, s_len, B_VDIM + LANES), BF16)],
        compiler_params=_params(("parallel", "parallel")),
        name="attn_prompt",
    )(q, k, v, zb)


def _attn_sample_kernel(q_ref, kp_ref, kn_ref, vp_ref, vn_ref, zb_ref, o_ref, *, heads):
    hs = range(heads)
    qs = [slice(hh * HEAD_PAD, (hh + 1) * HEAD_PAD) for hh in hs]
    vs = [slice(hh * B_VDIM, (hh + 1) * B_VDIM) for hh in hs]
    q = [q_ref[:, qs[hh]] for hh in hs]
    sp = [_qk(q[hh], kp_ref[:, qs[hh]]) * ATTN_SCALE for hh in hs]
    sn = [_qk(q[hh], kn_ref[:, qs[hh]]) * ATTN_SCALE for hh in hs]
    mx = [jnp.maximum(jnp.max(sp[hh], axis=-1, keepdims=True), jnp.max(sn[hh], axis=-1, keepdims=True))
          for hh in hs]
    pp = [jnp.exp(sp[hh] - mx[hh]) for hh in hs]
    pn = [jnp.exp(sn[hh] - mx[hh]) for hh in hs]
    denom = [jnp.sum(pp[hh], axis=-1, keepdims=True) + jnp.sum(pn[hh], axis=-1, keepdims=True)
             for hh in hs]
    o = [(_dot(pp[hh].astype(BF16), vp_ref[:, vs[hh]]) + _dot(pn[hh].astype(BF16), vn_ref[:, vs[hh]]))
         / denom[hh] for hh in hs]
    for hh in hs:
        o_ref[:, vs[hh]] = (o[hh] * zb_ref[:, vs[hh]]).astype(o_ref.dtype)


def _attn_sample(q, k_past, k_new, v_past, v_new, zb, heads):
    bn, t, _ = q.shape
    past = k_past.shape[1]
    blk = lambda n, w: pl.BlockSpec((None, n, heads * w), lambda b, h: (b, 0, h))
    return pl.pallas_call(
        functools.partial(_attn_sample_kernel, heads=heads),
        grid=(bn, B_HEADS // heads),
        in_specs=[blk(t, HEAD_PAD), blk(past, HEAD_PAD), blk(t, HEAD_PAD),
                  blk(past, B_VDIM), blk(t, B_VDIM), blk(t, B_VDIM)],
        out_specs=blk(t, B_VDIM),
        out_shape=jax.ShapeDtypeStruct((bn, t, B_WIDTH), BF16),
        compiler_params=_params(("parallel", "parallel")),
        name="attn_sample",
    )(q, k_past, k_new, v_past, v_new, zb)


def _merge_kernel(xn_ref, ya_ref, yb_ref, yc_ref, wg0_ref, wg1_ref, wg2_ref,
                  wpa_ref, wpb_ref, wpc_ref, h_ref):
    xn = xn_ref[...]
    h = (_sigmoid(_dot(xn, wg0_ref[...])) * _dot(ya_ref[...], wpa_ref[...])
         + _sigmoid(_dot(xn, wg1_ref[...])) * _dot(yb_ref[...], wpb_ref[...])
         + _sigmoid(_dot(xn, wg2_ref[...])) * _dot(yc_ref[...], wpc_ref[...]))
    h_ref[...] = h.astype(h_ref.dtype)


def _merge(xn, ya, yb, yc, p, l, tm, tn):
    m = xn.shape[0]
    nj = D_MODEL // tn
    row = lambda w: pl.BlockSpec((tm, w), lambda i, j: (i, 0))
    colw = lambda k, off: pl.BlockSpec((None, k, tn), lambda i, j: (l, 0, j + off))
    g0 = R_G // tn
    return pl.pallas_call(
        _merge_kernel,
        grid=(m // tm, nj),
        in_specs=[row(D_MODEL), row(A_WIDTH), row(B_WIDTH), row(C_WIDTH),
                  colw(D_MODEL, g0), colw(D_MODEL, g0 + nj), colw(D_MODEL, g0 + 2 * nj),
                  colw(A_WIDTH, 0), colw(B_WIDTH, 0), colw(C_WIDTH, 0)],
        out_specs=pl.BlockSpec((tm, tn), lambda i, j: (i, j)),
        out_shape=jax.ShapeDtypeStruct((m, D_MODEL), BF16),
        compiler_params=_params(("parallel", "parallel")),
        name="merge",
    )(xn, ya, yb, yc, p['w_in'], p['w_in'], p['w_in'], p['w_pa'], p['w_pb'], p['w_pc'])


def _out_kernel(x_ref, h_ref, w_ref, o_ref):
    o_ref[...] = x_ref[...] + _dot(h_ref[...], w_ref[...])


def _out_proj(x, h, p, l, tm):
    m = x.shape[0]
    return pl.pallas_call(
        _out_kernel,
        grid=(m // tm,),
        in_specs=[_rows(tm, D_MODEL), _rows(tm, D_MODEL), _layer_block(l, (D_MODEL, D_MODEL))],
        out_specs=_rows(tm, D_MODEL),
        out_shape=jax.ShapeDtypeStruct((m, D_MODEL), F32),
        compiler_params=_params(("parallel",)),
        name="out_proj",
    )(x, h, p['w_out'])


S_A = 0
S_CQKV = S_A + 3 * A_WIDTH
S_KPE = S_CQKV + B_Q_RANK + B_KV_RANK
S_BZ = S_KPE + B_ROPE
S_C = S_BZ + B_WIDTH
S_G = S_C + 4 * C_WIDTH
S_TOTAL = S_G + N_BRANCH * D_MODEL


def _relayout_kernel(w_ref, o_ref):
    rows = w_ref.shape[0]
    for dst, src, width in ((R_C, S_C, 4 * C_WIDTH), (R_BZ, S_BZ, B_WIDTH), (R_A, S_A, 3 * A_WIDTH),
                            (R_CQKV, S_CQKV, B_Q_RANK + B_KV_RANK), (R_KPE, S_KPE, B_ROPE),
                            (R_G, S_G, N_BRANCH * D_MODEL)):
        o_ref[:, dst:dst + width] = w_ref[:, src:src + width].astype(o_ref.dtype)
    o_ref[:, R_KPE + B_ROPE:R_G] = jnp.zeros((rows, R_G - R_KPE - B_ROPE), o_ref.dtype)


def _relayout_w_in(w_in, tr):
    depth = w_in.shape[0]
    assert w_in.shape[2] == S_TOTAL
    return pl.pallas_call(
        _relayout_kernel,
        grid=(depth, D_MODEL // tr),
        in_specs=[pl.BlockSpec((None, tr, S_TOTAL), lambda l, i: (l, i, 0))],
        out_specs=pl.BlockSpec((None, tr, R_TOTAL), lambda l, i: (l, i, 0)),
        out_shape=jax.ShapeDtypeStruct((depth, D_MODEL, R_TOTAL), BF16),
        compiler_params=_params(("parallel", "parallel")),
        name="relayout_w_in",
    )(w_in)


def _rope_tables(pos):
    half = B_ROPE // 2
    inv = ROPE_BASE ** (-jnp.arange(half, dtype=F32) / half)
    ang = pos[:, None] * inv[None, :]
    c, s = jnp.cos(ang), jnp.sin(ang)
    z = jnp.zeros_like(c)
    return (jnp.concatenate([c, c, z, z], axis=1),
            jnp.concatenate([-s, z, z, z], axis=1),
            jnp.concatenate([z, s, z, z], axis=1))


def _prepare_params(norm_gain, w_in, a_v_gain, a_ws, a_bias, b_q_gain, b_w_qb, b_kv_gain, b_kr_gain,
                    b_w_kvb, b_qn_gain, b_qr_gain, b_kn_gain, c_conv_w, w_pa, w_pb, w_pc, w_out):
    depth = w_in.shape[0]
    w_r = _relayout_w_in(w_in, 128)
    wqb = jnp.pad(b_w_qb.reshape(depth, B_Q_RANK, B_HEADS, B_QK),
                  ((0, 0), (0, 0), (0, 0), (0, HEAD_PAD - B_QK))).reshape(depth, B_Q_RANK, QK_WIDTH)
    wkv = b_w_kvb.reshape(depth, B_KV_RANK, B_HEADS, B_NOPE + B_VDIM)
    wkvb = jnp.concatenate([wkv[..., :B_NOPE].reshape(depth, B_KV_RANK, B_HEADS * B_NOPE),
                            wkv[..., B_NOPE:].reshape(depth, B_KV_RANK, B_WIDTH)], axis=2)
    row = lambda g: g[:, None, :]
    pad_rope = lambda g: jnp.pad(g, ((0, 0), (0, LANES - B_ROPE)))[:, None, :]
    return dict(
        norm=row(norm_gain), w_in=w_r, a_v_gain=row(a_v_gain), a_ws=a_ws, a_bias=a_bias,
        wqb=wqb.astype(BF16), wkvb=wkvb.astype(BF16),
        b_q_gain=row(b_q_gain), b_kv_gain=row(b_kv_gain), b_kr_gain=pad_rope(b_kr_gain),
        b_qn_gain=row(b_qn_gain), b_qr_gain=pad_rope(b_qr_gain), b_kn_gain=row(b_kn_gain),
        conv_w=c_conv_w,
        w_pa=w_pa.astype(BF16), w_pb=w_pb.astype(BF16), w_pc=w_pc.astype(BF16),
        w_out=w_out.astype(BF16))


def _finish_layer(x, xn, ya, yb, yc, p, l, tm_merge, tm_out):
    h = _merge(xn, ya, yb, yc, p, l, tm_merge, 256)
    return _out_proj(x, h, p, l, tm_out)


def _prompt_layer(x, p, l, tables, bn, s_len, latents, zero_state):
    tm = 512
    xn = _norm(x, p['norm'], l, tm)
    (ya,) = _branch_a(xn, p, l, p['a_ws'], jnp.swapaxes(p['a_bias'], 1, 2), tm, False)
    yc, conv_new = _branch_c(xn, p, l, zero_state, 0, tm, tm, s_len // tm)
    tm_b = 256
    q, k, v, zb, ckv, kpe = _branch_b(xn, p, l, tables, tm_b, s_len // tm_b, latents)
    r3 = lambda a: a.reshape(bn, s_len, a.shape[-1])
    yb = _attn_prompt(r3(q), r3(k), r3(v), r3(zb), 256, 4).reshape(bn * s_len, B_WIDTH)
    out = _finish_layer(x, xn, ya, yb, yc, p, l, 1024, tm)
    return out, (ckv, kpe), conv_new


def _sample_layer(x, p, l, tables, bn, t, ws_blk, bias_t, state_conv, cache_ckv, cache_kpe, latents):
    m = bn * t
    past = cache_ckv.shape[1] // bn
    xn = _norm(x, p['norm'], l, m)
    ya, a_vn = _branch_a(xn, p, l, ws_blk, bias_t, m, True)
    yc, conv_new = _branch_c(xn, p, l, state_conv, l, m, t, 0)
    q, k, v, zb, ckv, kpe = _branch_b(xn, p, l, tables, m, 1, latents)
    k_past, v_past = _expand_cache(cache_ckv, cache_kpe, p, l, 512)
    r3 = lambda a, n: a.reshape(bn, n, a.shape[-1])
    yb = _attn_sample(r3(q, t), r3(k_past, past), r3(k, t), r3(v_past, past), r3(v, t),
                      r3(zb, t), 8).reshape(m, B_WIDTH)
    out = _finish_layer(x, xn, ya, yb, yc, p, l, m, m)
    return out, (ckv, kpe), conv_new, a_vn


def kernel(x_prompt, x_sample, cache_ckv, cache_kpe, state_conv, norm_gain, w_in, a_v_gain, a_ws, a_bias,
           b_q_gain, b_w_qb, b_kv_gain, b_kr_gain, b_w_kvb, b_qn_gain, b_qr_gain, b_kn_gain, c_conv_w,
           w_pa, w_pb, w_pc, w_out):
    depth = w_in.shape[0]
    bn, s_len, _ = x_prompt.shape
    dbn, t, _ = x_sample.shape
    past = cache_ckv.shape[2]
    assert A_CHUNK % t == 0 and dbn * t == A_CHUNK and past % A_CHUNK == 0

    p = _prepare_params(norm_gain, w_in, a_v_gain, a_ws, a_bias, b_q_gain, b_w_qb, b_kv_gain, b_kr_gain,
                        b_w_kvb, b_qn_gain, b_qr_gain, b_kn_gain, c_conv_w, w_pa, w_pb, w_pc, w_out)
    tab_p = _rope_tables(jnp.arange(s_len, dtype=F32))
    tab_s = tuple(jnp.tile(tb, (dbn, 1)) for tb in _rope_tables(past + jnp.arange(t, dtype=F32)))

    hp = x_prompt.reshape(bn * s_len, D_MODEL)
    zero_state = jnp.zeros((1, bn, C_CONV - 1, C_WIDTH), F32)
    lat_p, conv_p = None, []
    for l in range(depth):
        hp, lat_p, cst = _prompt_layer(hp, p, l, tab_p, bn, s_len, lat_p, zero_state)
        conv_p.append(cst)

    m_s = dbn * t
    ws_blk = jnp.einsum('ab,lgts->lgatbs', jnp.eye(dbn, dtype=F32), a_ws[:, :, :t, :t]).reshape(
        depth, A_GROUPS, m_s, m_s)
    bias_s = jnp.swapaxes(jnp.tile(a_bias[:, :, :t], (1, 1, dbn)), 1, 2)
    ckv_c = cache_ckv.reshape(depth, dbn * past, B_KV_RANK)
    kpe_c = cache_kpe.reshape(depth, dbn * past, B_ROPE)
    hs = x_sample.reshape(m_s, D_MODEL)
    lat_s, conv_s, av_s = None, [], []
    for l in range(depth):
        hs, lat_s, cst, avn = _sample_layer(hs, p, l, tab_s, dbn, t, ws_blk, bias_s, state_conv,
                                            ckv_c, kpe_c, lat_s)
        conv_s.append(cst)
        av_s.append(avn.reshape(dbn, t, A_WIDTH))

    return (hp.reshape(bn, s_len, D_MODEL), hs.reshape(dbn, t, D_MODEL),
            lat_p[0].reshape(depth, bn, s_len, B_KV_RANK), lat_p[1].reshape(depth, bn, s_len, B_ROPE),
            jnp.stack(conv_p),
            lat_s[0].reshape(depth, dbn, t, B_KV_RANK), lat_s[1].reshape(depth, dbn, t, B_ROPE),
            jnp.stack(conv_s), jnp.stack(av_s))
```

```python
import functools
import math

import jax
import jax.numpy as jnp
from jax import lax
from jax.experimental import pallas as pl
from jax.experimental.pallas import tpu as pltpu

D_MODEL = 2048
CHUNK = 64
EPS = 1e-6
A_WIDTH = 1024
A_GROUPS = 4
A_GROUP_DIM = A_WIDTH // A_GROUPS
A_CHUNK = 128
B_HEADS = 16
B_Q_RANK = 512
B_KV_RANK = 512
B_NOPE = 128
B_ROPE = 64
B_VDIM = 128
B_QK = B_NOPE + B_ROPE
B_WIDTH = B_HEADS * B_VDIM
ATTN_SCALE = B_QK ** -0.5
ROPE_BASE = 10000.0
C_WIDTH = 1024
C_CONV = 3
N_BRANCH = 3

LANES = 128
HEAD_PAD = 2 * LANES
QK_WIDTH = B_HEADS * HEAD_PAD

R_C = 0
R_BZ = R_C + 4 * C_WIDTH
R_A = R_BZ + B_WIDTH
R_CQKV = R_A + 3 * A_WIDTH
R_KPE = R_CQKV + B_Q_RANK + B_KV_RANK
R_G = R_KPE + 2 * LANES
R_TOTAL = R_G + N_BRANCH * D_MODEL

VMEM_LIMIT = 56 * 1024 * 1024
BF16 = jnp.bfloat16
F32 = jnp.float32


def _dot(a, b):
    return jnp.dot(a, b, preferred_element_type=F32)


def _rms(x, gain, width=None):
    width = x.shape[-1] if width is None else width
    ms = jnp.sum(x * x, axis=-1, keepdims=True) * (1.0 / width)
    return x * lax.rsqrt(ms + EPS) * gain


def _silu(x):
    return x * (1.0 / (1.0 + jnp.exp(-x)))


def _sigmoid(x):
    return 1.0 / (1.0 + jnp.exp(-x))


def _params(sem):
    return pltpu.CompilerParams(dimension_semantics=sem, vmem_limit_bytes=VMEM_LIMIT)


def _layer_block(l, shape, blk=0):
    nd = len(shape)
    return pl.BlockSpec((None,) + tuple(shape), lambda *_: (l,) + (0,) * (nd - 1) + (blk,),
                        pipeline_mode=pl.Buffered(1))


def _w_in_block(l, width, offset):
    return _layer_block(l, (D_MODEL, width), offset // width)


def _rows(tm, width):
    return pl.BlockSpec((tm, width), lambda i: (i, 0))


def _branch_a_kernel(x_ref, ng_ref, w_ref, gain_ref, ws_ref, bias_ref, xn_ref, ya_ref, *vn_ref, tm):
    xn = _rms(x_ref[...], ng_ref[...]).astype(xn_ref.dtype)
    xn_ref[...] = xn
    row = lax.broadcasted_iota(jnp.int32, (A_CHUNK, A_CHUNK), 0)
    col = lax.broadcasted_iota(jnp.int32, (A_CHUNK, A_CHUNK), 1)
    for g in range(A_GROUPS):
        lo, hi = g * A_GROUP_DIM, (g + 1) * A_GROUP_DIM
        u = _dot(xn, w_ref[:, lo:hi])
        v = _dot(xn, w_ref[:, A_WIDTH + lo:A_WIDTH + hi])
        z = _dot(xn, w_ref[:, 2 * A_WIDTH + lo:2 * A_WIDTH + hi])
        vn = _rms(v, gain_ref[:, lo:hi])
        if vn_ref:
            vn_ref[0][:, lo:hi] = vn
        gate = u * _silu(z)
        ws = jnp.where(col <= row, ws_ref[g], 0.0).astype(BF16)
        bias = bias_ref[:, g:g + 1]
        vnb = vn.astype(BF16)
        for c in range(tm // A_CHUNK):
            r0, r1 = c * A_CHUNK, (c + 1) * A_CHUNK
            mixed = _dot(ws, vnb[r0:r1]) + bias
            ya_ref[r0:r1, lo:hi] = (gate[r0:r1] * mixed).astype(ya_ref.dtype)


def _branch_a(x, p, l, ws, bias_t, tm, emit_vn):
    m = x.shape[0]
    out_shape = [jax.ShapeDtypeStruct((m, D_MODEL), BF16), jax.ShapeDtypeStruct((m, A_WIDTH), BF16)]
    out_specs = [_rows(tm, D_MODEL), _rows(tm, A_WIDTH)]
    if emit_vn:
        out_shape.append(jax.ShapeDtypeStruct((m, A_WIDTH), F32))
        out_specs.append(_rows(tm, A_WIDTH))
    return pl.pallas_call(
        functools.partial(_branch_a_kernel, tm=tm),
        grid=(m // tm,),
        in_specs=[_rows(tm, D_MODEL), _layer_block(l, (1, D_MODEL)), _w_in_block(l, 3 * A_WIDTH, R_A),
                  _layer_block(l, (1, A_WIDTH)), _layer_block(l, ws.shape[1:]),
                  _layer_block(l, bias_t.shape[1:])],
        out_specs=out_specs,
        out_shape=out_shape,
        compiler_params=_params(("parallel",)),
        name="branch_a",
    )(x, p['norm'], p['w_in'], p['a_v_gain'], ws, bias_t)


def _branch_c_kernel(xn_ref, w_ref, cw_ref, state_ref, yc_ref, new_ref, pad_ref, *, seg, carry):
    tm = xn_ref.shape[0]
    xn = xn_ref[...]
    h = _dot(xn, w_ref[:, 0:C_WIDTH])
    b = _dot(xn, w_ref[:, C_WIDTH:2 * C_WIDTH])
    c = _dot(xn, w_ref[:, 2 * C_WIDTH:3 * C_WIDTH])
    z = _dot(xn, w_ref[:, 3 * C_WIDTH:4 * C_WIDTH])
    hc = c * h
    gate = b * _silu(z)
    w0, w1, w2 = cw_ref[0:1, :], cw_ref[1:2, :], cw_ref[2:3, :]
    stride = seg + 8
    if carry:
        first = pl.program_id(0) % carry == 0

        @pl.when(first)
        def _():
            pad_ref[6:8, :] = state_ref[0]
    for s in range(tm // seg):
        base = s * stride
        if not carry:
            pad_ref[base + 6:base + 8, :] = state_ref[s]
        pad_ref[base + 8:base + 8 + seg, :] = hc[s * seg:(s + 1) * seg]
        y = (w2 * hc[s * seg:(s + 1) * seg]
             + w1 * pad_ref[base + 7:base + 7 + seg, :]
             + w0 * pad_ref[base + 6:base + 6 + seg, :])
        yc_ref[s * seg:(s + 1) * seg, :] = (gate[s * seg:(s + 1) * seg] * y).astype(yc_ref.dtype)
        new_ref[s] = hc[(s + 1) * seg - 2:(s + 1) * seg]
    if carry:
        pad_ref[6:8, :] = hc[tm - 2:tm]


def _branch_c(xn, p, l, state, state_layer, tm, seg, tiles_per_stream):
    m = xn.shape[0]
    n_streams = state.shape[1]
    nseg = tm // seg
    blk = (nseg, C_CONV - 1, C_WIDTH)
    if tiles_per_stream:
        st_in = pl.BlockSpec((None,) + blk, lambda i: (state_layer, i // tiles_per_stream, 0, 0))
        st_out = pl.BlockSpec(blk, lambda i: (i // tiles_per_stream, 0, 0))
    else:
        st_in = pl.BlockSpec((None,) + blk, lambda i: (state_layer, i, 0, 0))
        st_out = pl.BlockSpec(blk, lambda i: (i, 0, 0))
    return pl.pallas_call(
        functools.partial(_branch_c_kernel, seg=seg, carry=tiles_per_stream),
        grid=(m // tm,),
        in_specs=[_rows(tm, D_MODEL), _w_in_block(l, 4 * C_WIDTH, R_C),
                  _layer_block(l, (C_CONV, C_WIDTH)), st_in],
        out_specs=[_rows(tm, C_WIDTH), st_out],
        out_shape=[jax.ShapeDtypeStruct((m, C_WIDTH), BF16),
                   jax.ShapeDtypeStruct((n_streams, C_CONV - 1, C_WIDTH), F32)],
        scratch_shapes=[pltpu.VMEM((nseg * (seg + 8), C_WIDTH), F32)],
        compiler_params=_params(("arbitrary",)),
        name="branch_c",
    )(xn, p['w_in'], p['conv_w'], state)


def _rope_padded(x, cos_ref, sin_lo_ref, sin_hi_ref):
    return (x * cos_ref[...]
            + pltpu.roll(x, LANES - B_ROPE // 2, 1) * sin_lo_ref[...]
            + pltpu.roll(x, B_ROPE // 2, 1) * sin_hi_ref[...])


def _expand_kv(ckv_n, kpe_pad, wkvb_ref, kng_ref, k_ref, v_ref):
    kv = _dot(ckv_n.astype(BF16), wkvb_ref[...])
    v_ref[...] = kv[:, B_HEADS * B_NOPE:].astype(v_ref.dtype)
    kpe_b = kpe_pad.astype(k_ref.dtype)
    kn = [_rms(kv[:, hd * B_NOPE:(hd + 1) * B_NOPE], kng_ref[...]) for hd in range(B_HEADS)]
    for hd in range(B_HEADS):
        k_ref[:, hd * HEAD_PAD:hd * HEAD_PAD + LANES] = kn[hd].astype(k_ref.dtype)
        k_ref[:, hd * HEAD_PAD + LANES:(hd + 1) * HEAD_PAD] = kpe_b


def _branch_b_kernel(xn_ref, wcqkv_ref, wkpe_ref, wbz_ref, wqb_ref, wkvb_ref, qg_ref, kvg_ref, krg_ref,
                     qng_ref, qrg_ref, kng_ref, cos_ref, sin_lo_ref, sin_hi_ref, *rest):
    q_ref, k_ref, v_ref, zb_ref, ckv_ref, kpe_ref = rest[-6:]
    xn = xn_ref[...]
    cq = _rms(_dot(xn, wcqkv_ref[:, 0:B_Q_RANK]), qg_ref[...])
    q = _dot(cq.astype(BF16), wqb_ref[...])
    heads = range(B_HEADS)
    qa = [_rms(q[:, hd * HEAD_PAD:hd * HEAD_PAD + LANES], qng_ref[...]) for hd in heads]
    qb = [_rms(q[:, hd * HEAD_PAD + LANES:(hd + 1) * HEAD_PAD], qrg_ref[...], B_ROPE) for hd in heads]
    qb = [_rope_padded(qb[hd], cos_ref, sin_lo_ref, sin_hi_ref) for hd in heads]
    for hd in heads:
        q_ref[:, hd * HEAD_PAD:hd * HEAD_PAD + LANES] = qa[hd].astype(q_ref.dtype)
        q_ref[:, hd * HEAD_PAD + LANES:(hd + 1) * HEAD_PAD] = qb[hd].astype(q_ref.dtype)
    ckv_n = _rms(_dot(xn, wcqkv_ref[:, B_Q_RANK:B_Q_RANK + B_KV_RANK]), kvg_ref[...])
    ckv_ref[...] = ckv_n
    kpe_pad = _rope_padded(_rms(_dot(xn, wkpe_ref[...]), krg_ref[...], B_ROPE),
                           cos_ref, sin_lo_ref, sin_hi_ref)
    kpe_ref[...] = kpe_pad[:, 0:B_ROPE]
    _expand_kv(ckv_n, kpe_pad, wkvb_ref, kng_ref, k_ref, v_ref)
    zb_ref[...] = _silu(_dot(xn, wbz_ref[...]))


def _branch_b(xn, p, l, tables, tm, table_tiles, latents):
    m = xn.shape[0]
    depth = p['w_in'].shape[0]
    tab = pl.BlockSpec((tm, LANES), lambda i: (i % table_tiles, 0))
    gains = [p[n] for n in ('b_q_gain', 'b_kv_gain', 'b_kr_gain', 'b_qn_gain', 'b_qr_gain', 'b_kn_gain')]
    acc = [] if latents is None else list(latents)
    n_in = 15
    return pl.pallas_call(
        _branch_b_kernel,
        grid=(m // tm,),
        in_specs=[_rows(tm, D_MODEL),
                  _w_in_block(l, B_Q_RANK + B_KV_RANK, R_CQKV), _w_in_block(l, LANES, R_KPE),
                  _w_in_block(l, B_WIDTH, R_BZ),
                  _layer_block(l, (B_Q_RANK, QK_WIDTH)), _layer_block(l, (B_KV_RANK, 2 * B_WIDTH))]
                 + [_layer_block(l, g.shape[1:]) for g in gains] + [tab, tab, tab]
                 + [pl.BlockSpec(memory_space=pl.ANY)] * len(acc),
        out_specs=[_rows(tm, QK_WIDTH), _rows(tm, QK_WIDTH), _rows(tm, B_WIDTH), _rows(tm, B_WIDTH),
                   pl.BlockSpec((None, tm, B_KV_RANK), lambda i: (l, i, 0)),
                   pl.BlockSpec((None, tm, B_ROPE), lambda i: (l, i, 0))],
        out_shape=[jax.ShapeDtypeStruct((m, QK_WIDTH), BF16),
                   jax.ShapeDtypeStruct((m, QK_WIDTH), BF16),
                   jax.ShapeDtypeStruct((m, B_WIDTH), BF16),
                   jax.ShapeDtypeStruct((m, B_WIDTH), F32),
                   jax.ShapeDtypeStruct((depth, m, B_KV_RANK), F32),
                   jax.ShapeDtypeStruct((depth, m, B_ROPE), F32)],
        input_output_aliases={n_in + j: 4 + j for j in range(len(acc))},
        compiler_params=_params(("parallel",)),
        name="branch_b",
    )(xn, p['w_in'], p['w_in'], p['w_in'], p['wqb'], p['wkvb'], *gains, *tables, *acc)


def _expand_cache_kernel(ckv_ref, kpe_ref, wkvb_ref, kng_ref, k_ref, v_ref, pad_ref):
    rows = kpe_ref.shape[0]
    pad_ref[:, 0:B_ROPE] = kpe_ref[...]
    pad_ref[:, B_ROPE:] = jnp.zeros((rows, LANES - B_ROPE), pad_ref.dtype)
    _expand_kv(ckv_ref[...], pad_ref[...], wkvb_ref, kng_ref, k_ref, v_ref)


def _expand_cache(ckv, kpe, p, l, tm):
    m = ckv.shape[1]
    lrows = lambda w: pl.BlockSpec((None, tm, w), lambda i: (l, i, 0))
    return pl.pallas_call(
        _expand_cache_kernel,
        grid=(m // tm,),
        in_specs=[lrows(B_KV_RANK), lrows(B_ROPE), _layer_block(l, (B_KV_RANK, 2 * B_WIDTH)),
                  _layer_block(l, (1, B_NOPE))],
        out_specs=[_rows(tm, QK_WIDTH), _rows(tm, B_WIDTH)],
        out_shape=[jax.ShapeDtypeStruct((m, QK_WIDTH), BF16), jax.ShapeDtypeStruct((m, B_WIDTH), BF16)],
        scratch_shapes=[pltpu.VMEM((tm, LANES), F32)],
        compiler_params=_params(("parallel",)),
        name="expand_cache",
    )(ckv, kpe, p['wkvb'], p['b_kn_gain'])


def _qk(q, k):
    return lax.dot_general(q, k, (((1,), (1,)), ((), ())), preferred_element_type=F32)


EXP2_SCALE = ATTN_SCALE * math.log2(math.e)


def _attn_prompt_kernel(q_ref, k_ref, v_ref, zb_ref, o_ref, vext_ref, *, tq, heads):
    s_len = q_ref.shape[0]
    rowc = lax.broadcasted_iota(jnp.int32, (tq, tq), 0) // CHUNK
    colc = lax.broadcasted_iota(jnp.int32, (tq, tq), 1) // CHUNK
    visible = colc <= rowc
    for hh in range(heads):
        vext_ref[hh, :, 0:B_VDIM] = v_ref[:, hh * B_VDIM:(hh + 1) * B_VDIM]
        vext_ref[hh, :, B_VDIM:] = jnp.ones((s_len, LANES), vext_ref.dtype)
    hs = range(heads)
    qs = [slice(hh * HEAD_PAD, (hh + 1) * HEAD_PAD) for hh in hs]
    vs = [slice(hh * B_VDIM, (hh + 1) * B_VDIM) for hh in hs]

    def scores(qi):
        r0 = qi * tq
        q_t = [q_ref[r0:r0 + tq, qs[hh]] for hh in hs]
        s_diag = [jnp.where(visible, _qk(q_t[hh], k_ref[r0:r0 + tq, qs[hh]]), -jnp.inf) for hh in hs]
        mx = [jnp.max(s_diag[hh], axis=-1, keepdims=True) for hh in hs]
        s_prev = None
        if qi:
            s_prev = [_qk(q_t[hh], k_ref[0:r0, qs[hh]]) for hh in hs]
            mx = [jnp.maximum(mx[hh], jnp.max(s_prev[hh], axis=-1, keepdims=True)) for hh in hs]
        return s_diag, s_prev, mx

    def weighted_values(qi, s_diag, s_prev, mx):
        r0 = qi * tq
        mc = [mx[hh] * EXP2_SCALE for hh in hs]
        p_diag = [jnp.exp2(s_diag[hh] * EXP2_SCALE - mc[hh]).astype(BF16) for hh in hs]
        acc = [_dot(p_diag[hh], vext_ref[hh, r0:r0 + tq, :]) for hh in hs]
        if qi:
            p_prev = [jnp.exp2(s_prev[hh] * EXP2_SCALE - mc[hh]).astype(BF16) for hh in hs]
            acc = [acc[hh] + _dot(p_prev[hh], vext_ref[hh, 0:r0, :]) for hh in hs]
        for hh in hs:
            o = acc[hh][:, 0:B_VDIM] / acc[hh][:, B_VDIM:]
            o_ref[r0:r0 + tq, vs[hh]] = (o * zb_ref[r0:r0 + tq, vs[hh]]).astype(o_ref.dtype)

    for qi in range(s_len // tq):
        weighted_values(qi, *scores(qi))


def _attn_prompt(q, k, v, zb, tq, heads):
    bn, s_len, _ = q.shape
    blk = lambda w: pl.BlockSpec((None, s_len, heads * w), lambda b, h: (b, 0, h))
    return pl.pallas_call(
        functools.partial(_attn_prompt_kernel, tq=tq, heads=heads),
        grid=(bn, B_HEADS // heads),
        in_specs=[blk(HEAD_PAD), blk(HEAD_PAD), blk(B_VDIM), blk(B_VDIM)],
        out_specs=blk(B_VDIM),
        out_shape=jax.ShapeDtypeStruct((bn, s_len, B_WIDTH), BF16),
        scratch_shapes=[pltpu.VMEM((heads, s_len, B_VDIM + LANES), BF16)],
        compiler_params=_params(("parallel", "parallel")),
        name="attn_prompt",
    )(q, k, v, zb)


def _attn_sample_kernel(q_ref, kp_ref, kn_ref, vp_ref, vn_ref, zb_ref, o_ref, *, heads):
    hs = range(heads)
    qs = [slice(hh * HEAD_PAD, (hh + 1) * HEAD_PAD) for hh in hs]
    vs = [slice(hh * B_VDIM, (hh + 1) * B_VDIM) for hh in hs]
    q = [q_ref[:, qs[hh]] for hh in hs]
    sp = [_qk(q[hh], kp_ref[:, qs[hh]]) * ATTN_SCALE for hh in hs]
    sn = [_qk(q[hh], kn_ref[:, qs[hh]]) * ATTN_SCALE for hh in hs]
    mx = [jnp.maximum(jnp.max(sp[hh], axis=-1, keepdims=True), jnp.max(sn[hh], axis=-1, keepdims=True))
          for hh in hs]
    pp = [jnp.exp(sp[hh] - mx[hh]) for hh in hs]
    pn = [jnp.exp(sn[hh] - mx[hh]) for hh in hs]
    denom = [jnp.sum(pp[hh], axis=-1, keepdims=True) + jnp.sum(pn[hh], axis=-1, keepdims=True)
             for hh in hs]
    o = [(_dot(pp[hh].astype(BF16), vp_ref[:, vs[hh]]) + _dot(pn[hh].astype(BF16), vn_ref[:, vs[hh]]))
         / denom[hh] for hh in hs]
    for hh in hs:
        o_ref[:, vs[hh]] = (o[hh] * zb_ref[:, vs[hh]]).astype(o_ref.dtype)


def _attn_sample(q, k_past, k_new, v_past, v_new, zb, heads):
    bn, t, _ = q.shape
    past = k_past.shape[1]
    blk = lambda n, w: pl.BlockSpec((None, n, heads * w), lambda b, h: (b, 0, h))
    return pl.pallas_call(
        functools.partial(_attn_sample_kernel, heads=heads),
        grid=(bn, B_HEADS // heads),
        in_specs=[blk(t, HEAD_PAD), blk(past, HEAD_PAD), blk(t, HEAD_PAD),
                  blk(past, B_VDIM), blk(t, B_VDIM), blk(t, B_VDIM)],
        out_specs=blk(t, B_VDIM),
        out_shape=jax.ShapeDtypeStruct((bn, t, B_WIDTH), BF16),
        compiler_params=_params(("parallel", "parallel")),
        name="attn_sample",
    )(q, k_past, k_new, v_past, v_new, zb)


def _merge_kernel(xn_ref, ya_ref, yb_ref, yc_ref, wg0_ref, wg1_ref, wg2_ref,
                  wpa_ref, wpb_ref, wpc_ref, h_ref):
    xn = xn_ref[...]
    h = (_sigmoid(_dot(xn, wg0_ref[...])) * _dot(ya_ref[...], wpa_ref[...])
         + _sigmoid(_dot(xn, wg1_ref[...])) * _dot(yb_ref[...], wpb_ref[...])
         + _sigmoid(_dot(xn, wg2_ref[...])) * _dot(yc_ref[...], wpc_ref[...]))
    h_ref[...] = h.astype(h_ref.dtype)


def _merge(xn, ya, yb, yc, p, l, tm, tn):
    m = xn.shape[0]
    nj = D_MODEL // tn
    row = lambda w: pl.BlockSpec((tm, w), lambda i, j: (i, 0))
    colw = lambda k, off: pl.BlockSpec((None, k, tn), lambda i, j: (l, 0, j + off))
    g0 = R_G // tn
    return pl.pallas_call(
        _merge_kernel,
        grid=(m // tm, nj),
        in_specs=[row(D_MODEL), row(A_WIDTH), row(B_WIDTH), row(C_WIDTH),
                  colw(D_MODEL, g0), colw(D_MODEL, g0 + nj), colw(D_MODEL, g0 + 2 * nj),
                  colw(A_WIDTH, 0), colw(B_WIDTH, 0), colw(C_WIDTH, 0)],
        out_specs=pl.BlockSpec((tm, tn), lambda i, j: (i, j)),
        out_shape=jax.ShapeDtypeStruct((m, D_MODEL), BF16),
        compiler_params=_params(("parallel", "parallel")),
        name="merge",
    )(xn, ya, yb, yc, p['w_in'], p['w_in'], p['w_in'], p['w_pa'], p['w_pb'], p['w_pc'])


def _out_kernel(x_ref, h_ref, w_ref, o_ref):
    o_ref[...] = x_ref[...] + _dot(h_ref[...], w_ref[...])


def _out_proj(x, h, p, l, tm):
    m = x.shape[0]
    return pl.pallas_call(
        _out_kernel,
        grid=(m // tm,),
        in_specs=[_rows(tm, D_MODEL), _rows(tm, D_MODEL), _layer_block(l, (D_MODEL, D_MODEL))],
        out_specs=_rows(tm, D_MODEL),
        out_shape=jax.ShapeDtypeStruct((m, D_MODEL), F32),
        compiler_params=_params(("parallel",)),
        name="out_proj",
    )(x, h, p['w_out'])


S_A = 0
S_CQKV = S_A + 3 * A_WIDTH
S_KPE = S_CQKV + B_Q_RANK + B_KV_RANK
S_BZ = S_KPE + B_ROPE
S_C = S_BZ + B_WIDTH
S_G = S_C + 4 * C_WIDTH
S_TOTAL = S_G + N_BRANCH * D_MODEL


RL_ROWS = 64
RL_PIECES = 4
RL_COLS = RL_ROWS * RL_PIECES
RL_SEGMENTS = ((R_C, S_C - R_C), (R_BZ, S_BZ - R_BZ), (R_A, S_A - R_A), (R_CQKV, S_CQKV - R_CQKV),
               (R_KPE, S_KPE - R_KPE), (R_G, S_G - R_G))
RL_KPE_BLOCK = R_KPE // RL_COLS


def _relayout_source_piece(j, k):
    delta = RL_SEGMENTS[0][1] // RL_ROWS
    for start, d in RL_SEGMENTS[1:]:
        delta = jnp.where(j >= start // RL_COLS, d // RL_ROWS, delta)
    return RL_PIECES * j + delta + k


def _relayout_kernel(*refs):
    o_ref = refs[-1]
    x = jnp.concatenate([r[...] for r in refs[:-1]], axis=0)
    row = lax.broadcasted_iota(jnp.int32, (RL_COLS, 1), 0)
    keep = jnp.logical_or(pl.program_id(1) != RL_KPE_BLOCK, row < B_ROPE)
    o_ref[...] = jnp.where(keep, x, 0.0).T.astype(o_ref.dtype)


def _relayout_w_in(w_in):
    depth = w_in.shape[0]
    assert w_in.shape[2] == S_TOTAL and R_TOTAL % RL_COLS == 0
    w_t = jnp.swapaxes(w_in, 1, 2)
    piece = lambda k: pl.BlockSpec((None, RL_ROWS, D_MODEL),
                                   lambda l, j: (l, _relayout_source_piece(j, k), 0))
    return pl.pallas_call(
        _relayout_kernel,
        grid=(depth, R_TOTAL // RL_COLS),
        in_specs=[piece(k) for k in range(RL_PIECES)],
        out_specs=pl.BlockSpec((None, D_MODEL, RL_COLS), lambda l, j: (l, 0, j)),
        out_shape=jax.ShapeDtypeStruct((depth, D_MODEL, R_TOTAL), BF16),
        compiler_params=_params(("parallel", "parallel")),
        name="relayout_w_in",
    )(*([w_t] * RL_PIECES))


def _rope_tables(pos):
    half = B_ROPE // 2
    inv = ROPE_BASE ** (-jnp.arange(half, dtype=F32) / half)
    ang = pos[:, None] * inv[None, :]
    c, s = jnp.cos(ang), jnp.sin(ang)
    z = jnp.zeros_like(c)
    return (jnp.concatenate([c, c, z, z], axis=1),
            jnp.concatenate([-s, z, z, z], axis=1),
            jnp.concatenate([z, s, z, z], axis=1))


def _prepare_params(norm_gain, w_in, a_v_gain, a_ws, a_bias, b_q_gain, b_w_qb, b_kv_gain, b_kr_gain,
                    b_w_kvb, b_qn_gain, b_qr_gain, b_kn_gain, c_conv_w, w_pa, w_pb, w_pc, w_out):
    depth = w_in.shape[0]
    w_r = _relayout_w_in(w_in)
    wqb = jnp.pad(b_w_qb.reshape(depth, B_Q_RANK, B_HEADS, B_QK),
                  ((0, 0), (0, 0), (0, 0), (0, HEAD_PAD - B_QK))).reshape(depth, B_Q_RANK, QK_WIDTH)
    wkv = b_w_kvb.reshape(depth, B_KV_RANK, B_HEADS, B_NOPE + B_VDIM)
    wkvb = jnp.concatenate([wkv[..., :B_NOPE].reshape(depth, B_KV_RANK, B_HEADS * B_NOPE),
                            wkv[..., B_NOPE:].reshape(depth, B_KV_RANK, B_WIDTH)], axis=2)
    row = lambda g: g[:, None, :]
    pad_rope = lambda g: jnp.pad(g, ((0, 0), (0, LANES - B_ROPE)))[:, None, :]
    return dict(
        norm=row(norm_gain), w_in=w_r, a_v_gain=row(a_v_gain), a_ws=a_ws, a_bias=a_bias,
        wqb=wqb.astype(BF16), wkvb=wkvb.astype(BF16),
        b_q_gain=row(b_q_gain), b_kv_gain=row(b_kv_gain), b_kr_gain=pad_rope(b_kr_gain),
        b_qn_gain=row(b_qn_gain), b_qr_gain=pad_rope(b_qr_gain), b_kn_gain=row(b_kn_gain),
        conv_w=c_conv_w,
        w_pa=w_pa.astype(BF16), w_pb=w_pb.astype(BF16), w_pc=w_pc.astype(BF16),
        w_out=w_out.astype(BF16))


def _finish_layer(x, xn, ya, yb, yc, p, l, tm_merge, tm_out):
    h = _merge(xn, ya, yb, yc, p, l, tm_merge, 256)
    return _out_proj(x, h, p, l, tm_out)


def _prompt_layer(x, p, l, tables, bn, s_len, latents, zero_state):
    tm = 512
    xn, ya = _branch_a(x, p, l, p['a_ws'], jnp.swapaxes(p['a_bias'], 1, 2), tm, False)
    yc, conv_new = _branch_c(xn, p, l, zero_state, 0, tm, tm, s_len // tm)
    tm_b = 256
    q, k, v, zb, ckv, kpe = _branch_b(xn, p, l, tables, tm_b, s_len // tm_b, latents)
    r3 = lambda a: a.reshape(bn, s_len, a.shape[-1])
    yb = _attn_prompt(r3(q), r3(k), r3(v), r3(zb), 256, 4).reshape(bn * s_len, B_WIDTH)
    out = _finish_layer(x, xn, ya, yb, yc, p, l, 1024, tm)
    return out, (ckv, kpe), conv_new


def _sample_layer(x, p, l, tables, bn, t, ws_blk, bias_t, state_conv, cache_ckv, cache_kpe, latents):
    m = bn * t
    past = cache_ckv.shape[1] // bn
    xn, ya, a_vn = _branch_a(x, p, l, ws_blk, bias_t, m, True)
    yc, conv_new = _branch_c(xn, p, l, state_conv, l, m, t, 0)
    q, k, v, zb, ckv, kpe = _branch_b(xn, p, l, tables, m, 1, latents)
    k_past, v_past = _expand_cache(cache_ckv, cache_kpe, p, l, 512)
    r3 = lambda a, n: a.reshape(bn, n, a.shape[-1])
    yb = _attn_sample(r3(q, t), r3(k_past, past), r3(k, t), r3(v_past, past), r3(v, t),
                      r3(zb, t), 8).reshape(m, B_WIDTH)
    out = _finish_layer(x, xn, ya, yb, yc, p, l, m, m)
    return out, (ckv, kpe), conv_new, a_vn


def kernel(x_prompt, x_sample, cache_ckv, cache_kpe, state_conv, norm_gain, w_in, a_v_gain, a_ws, a_bias,
           b_q_gain, b_w_qb, b_kv_gain, b_kr_gain, b_w_kvb, b_qn_gain, b_qr_gain, b_kn_gain, c_conv_w,
           w_pa, w_pb, w_pc, w_out):
    depth = w_in.shape[0]
    bn, s_len, _ = x_prompt.shape
    dbn, t, _ = x_sample.shape
    past = cache_ckv.shape[2]
    assert A_CHUNK % t == 0 and dbn * t == A_CHUNK and past % A_CHUNK == 0

    p = _prepare_params(norm_gain, w_in, a_v_gain, a_ws, a_bias, b_q_gain, b_w_qb, b_kv_gain, b_kr_gain,
                        b_w_kvb, b_qn_gain, b_qr_gain, b_kn_gain, c_conv_w, w_pa, w_pb, w_pc, w_out)
    tab_p = _rope_tables(jnp.arange(s_len, dtype=F32))
    tab_s = tuple(jnp.tile(tb, (dbn, 1)) for tb in _rope_tables(past + jnp.arange(t, dtype=F32)))

    hp = x_prompt.reshape(bn * s_len, D_MODEL)
    zero_state = jnp.zeros((1, bn, C_CONV - 1, C_WIDTH), F32)
    lat_p, conv_p = None, []
    for l in range(depth):
        hp, lat_p, cst = _prompt_layer(hp, p, l, tab_p, bn, s_len, lat_p, zero_state)
        conv_p.append(cst)

    m_s = dbn * t
    ws_blk = jnp.einsum('ab,lgts->lgatbs', jnp.eye(dbn, dtype=F32), a_ws[:, :, :t, :t]).reshape(
        depth, A_GROUPS, m_s, m_s)
    bias_s = jnp.swapaxes(jnp.tile(a_bias[:, :, :t], (1, 1, dbn)), 1, 2)
    ckv_c = cache_ckv.reshape(depth, dbn * past, B_KV_RANK)
    kpe_c = cache_kpe.reshape(depth, dbn * past, B_ROPE)
    hs = x_sample.reshape(m_s, D_MODEL)
    lat_s, conv_s, av_s = None, [], []
    for l in range(depth):
        hs, lat_s, cst, avn = _sample_layer(hs, p, l, tab_s, dbn, t, ws_blk, bias_s, state_conv,
                                            ckv_c, kpe_c, lat_s)
        conv_s.append(cst)
        av_s.append(avn.reshape(dbn, t, A_WIDTH))

    return (hp.reshape(bn, s_len, D_MODEL), hs.reshape(dbn, t, D_MODEL),
            lat_p[0].reshape(depth, bn, s_len, B_KV_RANK), lat_p[1].reshape(depth, bn, s_len, B_ROPE),
            jnp.stack(conv_p),
            lat_s[0].reshape(depth, dbn, t, B_KV_RANK), lat_s[1].reshape(depth, dbn, t, B_ROPE),
            jnp.stack(conv_s), jnp.stack(av_s))
```

```python
import functools
import math

import jax
import jax.numpy as jnp
from jax import lax
from jax.experimental import pallas as pl
from jax.experimental.pallas import tpu as pltpu

D_MODEL = 2048
CHUNK = 64
EPS = 1e-6
A_WIDTH = 1024
A_GROUPS = 4
A_GROUP_DIM = A_WIDTH // A_GROUPS
A_CHUNK = 128
B_HEADS = 16
B_Q_RANK = 512
B_KV_RANK = 512
B_NOPE = 128
B_ROPE = 64
B_VDIM = 128
B_QK = B_NOPE + B_ROPE
B_WIDTH = B_HEADS * B_VDIM
ATTN_SCALE = B_QK ** -0.5
ROPE_BASE = 10000.0
C_WIDTH = 1024
C_CONV = 3
N_BRANCH = 3

LANES = 128
HEAD_PAD = 2 * LANES
QK_WIDTH = B_HEADS * HEAD_PAD

R_C = 0
R_BZ = R_C + 4 * C_WIDTH
R_A = R_BZ + B_WIDTH
R_CQKV = R_A + 3 * A_WIDTH
R_KPE = R_CQKV + B_Q_RANK + B_KV_RANK
R_G = R_KPE + 4 * LANES
R_TOTAL = R_G + N_BRANCH * D_MODEL

VMEM_LIMIT = 56 * 1024 * 1024
BF16 = jnp.bfloat16
F32 = jnp.float32


def _dot(a, b):
    return jnp.dot(a, b, preferred_element_type=F32)


def _rms(x, gain, width=None):
    width = x.shape[-1] if width is None else width
    ms = jnp.sum(x * x, axis=-1, keepdims=True) * (1.0 / width)
    return x * lax.rsqrt(ms + EPS) * gain


def _silu(x):
    return x * (1.0 / (1.0 + jnp.exp(-x)))


def _sigmoid(x):
    return 1.0 / (1.0 + jnp.exp(-x))


def _params(sem):
    return pltpu.CompilerParams(dimension_semantics=sem, vmem_limit_bytes=VMEM_LIMIT)


def _layer_block(l, shape, blk=0):
    nd = len(shape)
    return pl.BlockSpec((None,) + tuple(shape), lambda *_: (l,) + (0,) * (nd - 1) + (blk,),
                        pipeline_mode=pl.Buffered(1))


def _w_in_block(l, width, offset):
    return _layer_block(l, (D_MODEL, width), offset // width)


def _rows(tm, width):
    return pl.BlockSpec((tm, width), lambda i: (i, 0))


def _branch_a_kernel(x_ref, ng_ref, w_ref, gain_ref, ws_ref, bias_ref, xn_ref, ya_ref, *vn_ref, tm):
    xn = _rms(x_ref[...], ng_ref[...]).astype(xn_ref.dtype)
    xn_ref[...] = xn
    row = lax.broadcasted_iota(jnp.int32, (A_CHUNK, A_CHUNK), 0)
    col = lax.broadcasted_iota(jnp.int32, (A_CHUNK, A_CHUNK), 1)
    for g in range(A_GROUPS):
        lo, hi = g * A_GROUP_DIM, (g + 1) * A_GROUP_DIM
        u = _dot(xn, w_ref[:, lo:hi])
        v = _dot(xn, w_ref[:, A_WIDTH + lo:A_WIDTH + hi])
        z = _dot(xn, w_ref[:, 2 * A_WIDTH + lo:2 * A_WIDTH + hi])
        vn = _rms(v, gain_ref[:, lo:hi])
        if vn_ref:
            vn_ref[0][:, lo:hi] = vn
        gate = u * _silu(z)
        ws = jnp.where(col <= row, ws_ref[g], 0.0).astype(BF16)
        bias = bias_ref[:, g:g + 1]
        vnb = vn.astype(BF16)
        for c in range(tm // A_CHUNK):
            r0, r1 = c * A_CHUNK, (c + 1) * A_CHUNK
            mixed = _dot(ws, vnb[r0:r1]) + bias
            ya_ref[r0:r1, lo:hi] = (gate[r0:r1] * mixed).astype(ya_ref.dtype)


def _branch_a(x, p, l, ws, bias_t, tm, emit_vn):
    m = x.shape[0]
    out_shape = [jax.ShapeDtypeStruct((m, D_MODEL), BF16), jax.ShapeDtypeStruct((m, A_WIDTH), BF16)]
    out_specs = [_rows(tm, D_MODEL), _rows(tm, A_WIDTH)]
    if emit_vn:
        out_shape.append(jax.ShapeDtypeStruct((m, A_WIDTH), F32))
        out_specs.append(_rows(tm, A_WIDTH))
    return pl.pallas_call(
        functools.partial(_branch_a_kernel, tm=tm),
        grid=(m // tm,),
        in_specs=[_rows(tm, D_MODEL), _layer_block(l, (1, D_MODEL)), _w_in_block(l, 3 * A_WIDTH, R_A),
                  _layer_block(l, (1, A_WIDTH)), _layer_block(l, ws.shape[1:]),
                  _layer_block(l, bias_t.shape[1:])],
        out_specs=out_specs,
        out_shape=out_shape,
        compiler_params=_params(("parallel",)),
        name="branch_a",
    )(x, p['norm'], p['w_in'], p['a_v_gain'], ws, bias_t)


def _branch_c_kernel(xn_ref, w_ref, cw_ref, state_ref, yc_ref, new_ref, pad_ref, *, seg, carry, cb):
    tm = xn_ref.shape[0]
    xn = xn_ref[...]
    stride = seg + 8
    if carry:
        first = pl.program_id(0) % carry == 0

        @pl.when(first)
        def _():
            pad_ref[6:8, :] = state_ref[0]
    for j in range(C_WIDTH // cb):
        cs = slice(j * cb, (j + 1) * cb)
        h, b, c, z = [_dot(xn, w_ref[:, k * C_WIDTH + j * cb:k * C_WIDTH + (j + 1) * cb])
                      for k in range(4)]
        hc = c * h
        gate = b * _silu(z)
        w0, w1, w2 = cw_ref[0:1, cs], cw_ref[1:2, cs], cw_ref[2:3, cs]
        for s in range(tm // seg):
            base = s * stride
            rows = slice(s * seg, (s + 1) * seg)
            if not carry:
                pad_ref[base + 6:base + 8, cs] = state_ref[s, :, cs]
            pad_ref[base + 8:base + 8 + seg, cs] = hc[rows]
            y = (w2 * hc[rows]
                 + w1 * pad_ref[base + 7:base + 7 + seg, cs]
                 + w0 * pad_ref[base + 6:base + 6 + seg, cs])
            yc_ref[rows, cs] = (gate[rows] * y).astype(yc_ref.dtype)
            new_ref[s, :, cs] = hc[(s + 1) * seg - 2:(s + 1) * seg]
        if carry:
            pad_ref[6:8, cs] = hc[tm - 2:tm]


def _branch_c(xn, p, l, state, state_layer, tm, seg, tiles_per_stream, cb):
    m = xn.shape[0]
    n_streams = state.shape[1]
    nseg = tm // seg
    blk = (nseg, C_CONV - 1, C_WIDTH)
    if tiles_per_stream:
        st_in = pl.BlockSpec((None,) + blk, lambda i: (state_layer, i // tiles_per_stream, 0, 0))
        st_out = pl.BlockSpec(blk, lambda i: (i // tiles_per_stream, 0, 0))
    else:
        st_in = pl.BlockSpec((None,) + blk, lambda i: (state_layer, i, 0, 0))
        st_out = pl.BlockSpec(blk, lambda i: (i, 0, 0))
    return pl.pallas_call(
        functools.partial(_branch_c_kernel, seg=seg, carry=tiles_per_stream, cb=cb),
        grid=(m // tm,),
        in_specs=[_rows(tm, D_MODEL), _w_in_block(l, 4 * C_WIDTH, R_C),
                  _layer_block(l, (C_CONV, C_WIDTH)), st_in],
        out_specs=[_rows(tm, C_WIDTH), st_out],
        out_shape=[jax.ShapeDtypeStruct((m, C_WIDTH), BF16),
                   jax.ShapeDtypeStruct((n_streams, C_CONV - 1, C_WIDTH), F32)],
        scratch_shapes=[pltpu.VMEM((nseg * (seg + 8), C_WIDTH), F32)],
        compiler_params=_params(("arbitrary",)),
        name="branch_c",
    )(xn, p['w_in'], p['conv_w'], state)


def _rope_padded(x, cos_ref, sin_lo_ref, sin_hi_ref):
    return (x * cos_ref[...]
            + pltpu.roll(x, LANES - B_ROPE // 2, 1) * sin_lo_ref[...]
            + pltpu.roll(x, B_ROPE // 2, 1) * sin_hi_ref[...])


def _expand_kv(ckv_n, kpe_pad, wkvb_ref, kng_ref, k_ref, v_ref):
    kv = _dot(ckv_n.astype(BF16), wkvb_ref[...])
    v_ref[...] = kv[:, B_HEADS * B_NOPE:].astype(v_ref.dtype)
    kpe_b = kpe_pad.astype(k_ref.dtype)
    kn = [_rms(kv[:, hd * B_NOPE:(hd + 1) * B_NOPE], kng_ref[...]) for hd in range(B_HEADS)]
    for hd in range(B_HEADS):
        k_ref[:, hd * HEAD_PAD:hd * HEAD_PAD + LANES] = kn[hd].astype(k_ref.dtype)
        k_ref[:, hd * HEAD_PAD + LANES:(hd + 1) * HEAD_PAD] = kpe_b


def _branch_b_kernel(xn_ref, wcqkv_ref, wkpe_ref, wbz_ref, wqb_ref, wkvb_ref, qg_ref, kvg_ref, krg_ref,
                     qng_ref, qrg_ref, kng_ref, cos_ref, sin_lo_ref, sin_hi_ref, *rest):
    q_ref, k_ref, v_ref, zb_ref, ckv_ref, kpe_ref = rest[-6:]
    xn = xn_ref[...]
    cq = _rms(_dot(xn, wcqkv_ref[:, 0:B_Q_RANK]), qg_ref[...])
    q = _dot(cq.astype(BF16), wqb_ref[...])
    heads = range(B_HEADS)
    qa = [_rms(q[:, hd * HEAD_PAD:hd * HEAD_PAD + LANES], qng_ref[...]) for hd in heads]
    qb = [_rms(q[:, hd * HEAD_PAD + LANES:(hd + 1) * HEAD_PAD], qrg_ref[...], B_ROPE) for hd in heads]
    qb = [_rope_padded(qb[hd], cos_ref, sin_lo_ref, sin_hi_ref) for hd in heads]
    for hd in heads:
        q_ref[:, hd * HEAD_PAD:hd * HEAD_PAD + LANES] = qa[hd].astype(q_ref.dtype)
        q_ref[:, hd * HEAD_PAD + LANES:(hd + 1) * HEAD_PAD] = qb[hd].astype(q_ref.dtype)
    ckv_n = _rms(_dot(xn, wcqkv_ref[:, B_Q_RANK:B_Q_RANK + B_KV_RANK]), kvg_ref[...])
    ckv_ref[...] = ckv_n
    kpe_pad = _rope_padded(_rms(_dot(xn, wkpe_ref[...]), krg_ref[...], B_ROPE),
                           cos_ref, sin_lo_ref, sin_hi_ref)
    kpe_ref[...] = kpe_pad[:, 0:B_ROPE]
    _expand_kv(ckv_n, kpe_pad, wkvb_ref, kng_ref, k_ref, v_ref)
    zb_ref[...] = _silu(_dot(xn, wbz_ref[...]))


def _branch_b(xn, p, l, tables, tm, table_tiles, latents):
    m = xn.shape[0]
    depth = p['w_in'].shape[0]
    tab = pl.BlockSpec((tm, LANES), lambda i: (i % table_tiles, 0))
    gains = [p[n] for n in ('b_q_gain', 'b_kv_gain', 'b_kr_gain', 'b_qn_gain', 'b_qr_gain', 'b_kn_gain')]
    acc = [] if latents is None else list(latents)
    n_in = 15
    return pl.pallas_call(
        _branch_b_kernel,
        grid=(m // tm,),
        in_specs=[_rows(tm, D_MODEL),
                  _w_in_block(l, B_Q_RANK + B_KV_RANK, R_CQKV), _w_in_block(l, LANES, R_KPE),
                  _w_in_block(l, B_WIDTH, R_BZ),
                  _layer_block(l, (B_Q_RANK, QK_WIDTH)), _layer_block(l, (B_KV_RANK, 2 * B_WIDTH))]
                 + [_layer_block(l, g.shape[1:]) for g in gains] + [tab, tab, tab]
                 + [pl.BlockSpec(memory_space=pl.ANY)] * len(acc),
        out_specs=[_rows(tm, QK_WIDTH), _rows(tm, QK_WIDTH), _rows(tm, B_WIDTH), _rows(tm, B_WIDTH),
                   pl.BlockSpec((None, tm, B_KV_RANK), lambda i: (l, i, 0)),
                   pl.BlockSpec((None, tm, B_ROPE), lambda i: (l, i, 0))],
        out_shape=[jax.ShapeDtypeStruct((m, QK_WIDTH), BF16),
                   jax.ShapeDtypeStruct((m, QK_WIDTH), BF16),
                   jax.ShapeDtypeStruct((m, B_WIDTH), BF16),
                   jax.ShapeDtypeStruct((m, B_WIDTH), F32),
                   jax.ShapeDtypeStruct((depth, m, B_KV_RANK), F32),
                   jax.ShapeDtypeStruct((depth, m, B_ROPE), F32)],
        input_output_aliases={n_in + j: 4 + j for j in range(len(acc))},
        compiler_params=_params(("parallel",)),
        name="branch_b",
    )(xn, p['w_in'], p['w_in'], p['w_in'], p['wqb'], p['wkvb'], *gains, *tables, *acc)


def _expand_cache_kernel(ckv_ref, kpe_ref, wkvb_ref, kng_ref, k_ref, v_ref, pad_ref):
    rows = kpe_ref.shape[0]
    pad_ref[:, 0:B_ROPE] = kpe_ref[...]
    pad_ref[:, B_ROPE:] = jnp.zeros((rows, LANES - B_ROPE), pad_ref.dtype)
    _expand_kv(ckv_ref[...], pad_ref[...], wkvb_ref, kng_ref, k_ref, v_ref)


def _expand_cache(ckv, kpe, p, l, tm):
    m = ckv.shape[1]
    lrows = lambda w: pl.BlockSpec((None, tm, w), lambda i: (l, i, 0))
    return pl.pallas_call(
        _expand_cache_kernel,
        grid=(m // tm,),
        in_specs=[lrows(B_KV_RANK), lrows(B_ROPE), _layer_block(l, (B_KV_RANK, 2 * B_WIDTH)),
                  _layer_block(l, (1, B_NOPE))],
        out_specs=[_rows(tm, QK_WIDTH), _rows(tm, B_WIDTH)],
        out_shape=[jax.ShapeDtypeStruct((m, QK_WIDTH), BF16), jax.ShapeDtypeStruct((m, B_WIDTH), BF16)],
        scratch_shapes=[pltpu.VMEM((tm, LANES), F32)],
        compiler_params=_params(("parallel",)),
        name="expand_cache",
    )(ckv, kpe, p['wkvb'], p['b_kn_gain'])


def _qk(q, k):
    return lax.dot_general(q, k, (((1,), (1,)), ((), ())), preferred_element_type=F32)


EXP2_SCALE = ATTN_SCALE * math.log2(math.e)


def _attn_prompt_kernel(q_ref, k_ref, v_ref, zb_ref, o_ref, vext_ref, *, tq, heads):
    s_len = q_ref.shape[0]
    rowc = lax.broadcasted_iota(jnp.int32, (tq, tq), 0) // CHUNK
    colc = lax.broadcasted_iota(jnp.int32, (tq, tq), 1) // CHUNK
    visible = colc <= rowc
    for hh in range(heads):
        vext_ref[hh, :, 0:B_VDIM] = v_ref[:, hh * B_VDIM:(hh + 1) * B_VDIM]
        vext_ref[hh, :, B_VDIM:] = jnp.ones((s_len, LANES), vext_ref.dtype)
    hs = range(heads)
    qs = [slice(hh * HEAD_PAD, (hh + 1) * HEAD_PAD) for hh in hs]
    vs = [slice(hh * B_VDIM, (hh + 1) * B_VDIM) for hh in hs]

    def scores(qi):
        r0 = qi * tq
        q_t = [q_ref[r0:r0 + tq, qs[hh]] for hh in hs]
        s_diag = [jnp.where(visible, _qk(q_t[hh], k_ref[r0:r0 + tq, qs[hh]]), -jnp.inf) for hh in hs]
        mx = [jnp.max(s_diag[hh], axis=-1, keepdims=True) for hh in hs]
        s_prev = None
        if qi:
            s_prev = [_qk(q_t[hh], k_ref[0:r0, qs[hh]]) for hh in hs]
            mx = [jnp.maximum(mx[hh], jnp.max(s_prev[hh], axis=-1, keepdims=True)) for hh in hs]
        return s_diag, s_prev, mx

    def weighted_values(qi, s_diag, s_prev, mx):
        r0 = qi * tq
        mc = [mx[hh] * EXP2_SCALE for hh in hs]
        p_diag = [jnp.exp2(s_diag[hh] * EXP2_SCALE - mc[hh]).astype(BF16) for hh in hs]
        acc = [_dot(p_diag[hh], vext_ref[hh, r0:r0 + tq, :]) for hh in hs]
        if qi:
            p_prev = [jnp.exp2(s_prev[hh] * EXP2_SCALE - mc[hh]).astype(BF16) for hh in hs]
            acc = [acc[hh] + _dot(p_prev[hh], vext_ref[hh, 0:r0, :]) for hh in hs]
        for hh in hs:
            o = acc[hh][:, 0:B_VDIM] / acc[hh][:, B_VDIM:]
            o_ref[r0:r0 + tq, vs[hh]] = (o * zb_ref[r0:r0 + tq, vs[hh]]).astype(o_ref.dtype)

    for qi in range(s_len // tq):
        weighted_values(qi, *scores(qi))


def _attn_prompt(q, k, v, zb, tq, heads):
    bn, s_len, _ = q.shape
    blk = lambda w: pl.BlockSpec((None, s_len, heads * w), lambda b, h: (b, 0, h))
    return pl.pallas_call(
        functools.partial(_attn_prompt_kernel, tq=tq, heads=heads),
        grid=(bn, B_HEADS // heads),
        in_specs=[blk(HEAD_PAD), blk(HEAD_PAD), blk(B_VDIM), blk(B_VDIM)],
        out_specs=blk(B_VDIM),
        out_shape=jax.ShapeDtypeStruct((bn, s_len, B_WIDTH), BF16),
        scratch_shapes=[pltpu.VMEM((heads, s_len, B_VDIM + LANES), BF16)],
        compiler_params=_params(("parallel", "parallel")),
        name="attn_prompt",
    )(q, k, v, zb)


def _attn_sample_kernel(q_ref, kp_ref, kn_ref, vp_ref, vn_ref, zb_ref, o_ref, *, heads):
    hs = range(heads)
    qs = [slice(hh * HEAD_PAD, (hh + 1) * HEAD_PAD) for hh in hs]
    vs = [slice(hh * B_VDIM, (hh + 1) * B_VDIM) for hh in hs]
    q = [q_ref[:, qs[hh]] for hh in hs]
    sp = [_qk(q[hh], kp_ref[:, qs[hh]]) * ATTN_SCALE for hh in hs]
    sn = [_qk(q[hh], kn_ref[:, qs[hh]]) * ATTN_SCALE for hh in hs]
    mx = [jnp.maximum(jnp.max(sp[hh], axis=-1, keepdims=True), jnp.max(sn[hh], axis=-1, keepdims=True))
          for hh in hs]
    pp = [jnp.exp(sp[hh] - mx[hh]) for hh in hs]
    pn = [jnp.exp(sn[hh] - mx[hh]) for hh in hs]
    denom = [jnp.sum(pp[hh], axis=-1, keepdims=True) + jnp.sum(pn[hh], axis=-1, keepdims=True)
             for hh in hs]
    o = [(_dot(pp[hh].astype(BF16), vp_ref[:, vs[hh]]) + _dot(pn[hh].astype(BF16), vn_ref[:, vs[hh]]))
         / denom[hh] for hh in hs]
    for hh in hs:
        o_ref[:, vs[hh]] = (o[hh] * zb_ref[:, vs[hh]]).astype(o_ref.dtype)


def _attn_sample(q, k_past, k_new, v_past, v_new, zb, heads):
    bn, t, _ = q.shape
    past = k_past.shape[1]
    blk = lambda n, w: pl.BlockSpec((None, n, heads * w), lambda b, h: (b, 0, h))
    return pl.pallas_call(
        functools.partial(_attn_sample_kernel, heads=heads),
        grid=(bn, B_HEADS // heads),
        in_specs=[blk(t, HEAD_PAD), blk(past, HEAD_PAD), blk(t, HEAD_PAD),
                  blk(past, B_VDIM), blk(t, B_VDIM), blk(t, B_VDIM)],
        out_specs=blk(t, B_VDIM),
        out_shape=jax.ShapeDtypeStruct((bn, t, B_WIDTH), BF16),
        compiler_params=_params(("parallel", "parallel")),
        name="attn_sample",
    )(q, k_past, k_new, v_past, v_new, zb)


def _merge_kernel(xn_ref, ya_ref, yb_ref, yc_ref, wg0_ref, wg1_ref, wg2_ref,
                  wpa_ref, wpb_ref, wpc_ref, h_ref):
    xn = xn_ref[...]
    h = (_sigmoid(_dot(xn, wg0_ref[...])) * _dot(ya_ref[...], wpa_ref[...])
         + _sigmoid(_dot(xn, wg1_ref[...])) * _dot(yb_ref[...], wpb_ref[...])
         + _sigmoid(_dot(xn, wg2_ref[...])) * _dot(yc_ref[...], wpc_ref[...]))
    h_ref[...] = h.astype(h_ref.dtype)


def _merge(xn, ya, yb, yc, p, l, tm, tn):
    m = xn.shape[0]
    nj = D_MODEL // tn
    row = lambda w: pl.BlockSpec((tm, w), lambda i, j: (i, 0))
    colw = lambda k, off: pl.BlockSpec((None, k, tn), lambda i, j: (l, 0, j + off))
    g0 = R_G // tn
    return pl.pallas_call(
        _merge_kernel,
        grid=(m // tm, nj),
        in_specs=[row(D_MODEL), row(A_WIDTH), row(B_WIDTH), row(C_WIDTH),
                  colw(D_MODEL, g0), colw(D_MODEL, g0 + nj), colw(D_MODEL, g0 + 2 * nj),
                  colw(A_WIDTH, 0), colw(B_WIDTH, 0), colw(C_WIDTH, 0)],
        out_specs=pl.BlockSpec((tm, tn), lambda i, j: (i, j)),
        out_shape=jax.ShapeDtypeStruct((m, D_MODEL), BF16),
        compiler_params=_params(("parallel", "parallel")),
        name="merge",
    )(xn, ya, yb, yc, p['w_in'], p['w_in'], p['w_in'], p['w_pa'], p['w_pb'], p['w_pc'])


def _out_kernel(x_ref, h_ref, w_ref, o_ref):
    o_ref[...] = x_ref[...] + _dot(h_ref[...], w_ref[...])


def _out_proj(x, h, p, l, tm):
    m = x.shape[0]
    return pl.pallas_call(
        _out_kernel,
        grid=(m // tm,),
        in_specs=[_rows(tm, D_MODEL), _rows(tm, D_MODEL), _layer_block(l, (D_MODEL, D_MODEL))],
        out_specs=_rows(tm, D_MODEL),
        out_shape=jax.ShapeDtypeStruct((m, D_MODEL), F32),
        compiler_params=_params(("parallel",)),
        name="out_proj",
    )(x, h, p['w_out'])


S_A = 0
S_CQKV = S_A + 3 * A_WIDTH
S_KPE = S_CQKV + B_Q_RANK + B_KV_RANK
S_BZ = S_KPE + B_ROPE
S_C = S_BZ + B_WIDTH
S_G = S_C + 4 * C_WIDTH
S_TOTAL = S_G + N_BRANCH * D_MODEL


RL_ROWS = 64
RL_PIECES = 8
RL_COLS = RL_ROWS * RL_PIECES
RL_SEGMENTS = ((R_C, S_C - R_C), (R_BZ, S_BZ - R_BZ), (R_A, S_A - R_A), (R_CQKV, S_CQKV - R_CQKV),
               (R_KPE, S_KPE - R_KPE), (R_G, S_G - R_G))
RL_KPE_BLOCK = R_KPE // RL_COLS


def _relayout_source_piece(j, k):
    delta = RL_SEGMENTS[0][1] // RL_ROWS
    for start, d in RL_SEGMENTS[1:]:
        delta = jnp.where(j >= start // RL_COLS, d // RL_ROWS, delta)
    return RL_PIECES * j + delta + k


def _relayout_kernel(*refs):
    o_ref = refs[-1]
    x = jnp.concatenate([r[...] for r in refs[:-1]], axis=0)
    row = lax.broadcasted_iota(jnp.int32, (RL_COLS, 1), 0)
    keep = jnp.logical_or(pl.program_id(1) != RL_KPE_BLOCK, row < B_ROPE)
    o_ref[...] = jnp.where(keep, x, 0.0).T.astype(o_ref.dtype)


def _relayout_w_in(w_in):
    depth = w_in.shape[0]
    assert w_in.shape[2] == S_TOTAL and R_TOTAL % RL_COLS == 0
    w_t = jnp.swapaxes(w_in, 1, 2)
    piece = lambda k: pl.BlockSpec((None, RL_ROWS, D_MODEL),
                                   lambda l, j: (l, _relayout_source_piece(j, k), 0))
    return pl.pallas_call(
        _relayout_kernel,
        grid=(depth, R_TOTAL // RL_COLS),
        in_specs=[piece(k) for k in range(RL_PIECES)],
        out_specs=pl.BlockSpec((None, D_MODEL, RL_COLS), lambda l, j: (l, 0, j)),
        out_shape=jax.ShapeDtypeStruct((depth, D_MODEL, R_TOTAL), BF16),
        compiler_params=_params(("parallel", "parallel")),
        name="relayout_w_in",
    )(*([w_t] * RL_PIECES))


def _rope_tables(pos):
    half = B_ROPE // 2
    inv = ROPE_BASE ** (-jnp.arange(half, dtype=F32) / half)
    ang = pos[:, None] * inv[None, :]
    c, s = jnp.cos(ang), jnp.sin(ang)
    z = jnp.zeros_like(c)
    return (jnp.concatenate([c, c, z, z], axis=1),
            jnp.concatenate([-s, z, z, z], axis=1),
            jnp.concatenate([z, s, z, z], axis=1))


def _prepare_params(norm_gain, w_in, a_v_gain, a_ws, a_bias, b_q_gain, b_w_qb, b_kv_gain, b_kr_gain,
                    b_w_kvb, b_qn_gain, b_qr_gain, b_kn_gain, c_conv_w, w_pa, w_pb, w_pc, w_out):
    depth = w_in.shape[0]
    w_r = _relayout_w_in(w_in)
    wqb = jnp.pad(b_w_qb.reshape(depth, B_Q_RANK, B_HEADS, B_QK),
                  ((0, 0), (0, 0), (0, 0), (0, HEAD_PAD - B_QK))).reshape(depth, B_Q_RANK, QK_WIDTH)
    wkv = b_w_kvb.reshape(depth, B_KV_RANK, B_HEADS, B_NOPE + B_VDIM)
    wkvb = jnp.concatenate([wkv[..., :B_NOPE].reshape(depth, B_KV_RANK, B_HEADS * B_NOPE),
                            wkv[..., B_NOPE:].reshape(depth, B_KV_RANK, B_WIDTH)], axis=2)
    row = lambda g: g[:, None, :]
    pad_rope = lambda g: jnp.pad(g, ((0, 0), (0, LANES - B_ROPE)))[:, None, :]
    return dict(
        norm=row(norm_gain), w_in=w_r, a_v_gain=row(a_v_gain), a_ws=a_ws, a_bias=a_bias,
        wqb=wqb.astype(BF16), wkvb=wkvb.astype(BF16),
        b_q_gain=row(b_q_gain), b_kv_gain=row(b_kv_gain), b_kr_gain=pad_rope(b_kr_gain),
        b_qn_gain=row(b_qn_gain), b_qr_gain=pad_rope(b_qr_gain), b_kn_gain=row(b_kn_gain),
        conv_w=c_conv_w,
        w_pa=w_pa.astype(BF16), w_pb=w_pb.astype(BF16), w_pc=w_pc.astype(BF16),
        w_out=w_out.astype(BF16))


def _finish_layer(x, xn, ya, yb, yc, p, l, tm_merge, tm_out):
    h = _merge(xn, ya, yb, yc, p, l, tm_merge, 256)
    return _out_proj(x, h, p, l, tm_out)


def _prompt_layer(x, p, l, tables, bn, s_len, latents, zero_state):
    tm = 1024
    xn, ya = _branch_a(x, p, l, p['a_ws'], jnp.swapaxes(p['a_bias'], 1, 2), tm, False)
    tm_c = 1024
    yc, conv_new = _branch_c(xn, p, l, zero_state, 0, tm_c, tm_c, s_len // tm_c, 512)
    tm_b = 256
    q, k, v, zb, ckv, kpe = _branch_b(xn, p, l, tables, tm_b, s_len // tm_b, latents)
    r3 = lambda a: a.reshape(bn, s_len, a.shape[-1])
    yb = _attn_prompt(r3(q), r3(k), r3(v), r3(zb), 256, 4).reshape(bn * s_len, B_WIDTH)
    out = _finish_layer(x, xn, ya, yb, yc, p, l, 1024, tm)
    return out, (ckv, kpe), conv_new


def _sample_layer(x, p, l, tables, bn, t, ws_blk, bias_t, state_conv, cache_ckv, cache_kpe, latents):
    m = bn * t
    past = cache_ckv.shape[1] // bn
    xn, ya, a_vn = _branch_a(x, p, l, ws_blk, bias_t, m, True)
    yc, conv_new = _branch_c(xn, p, l, state_conv, l, m, t, 0, C_WIDTH)
    q, k, v, zb, ckv, kpe = _branch_b(xn, p, l, tables, m, 1, latents)
    k_past, v_past = _expand_cache(cache_ckv, cache_kpe, p, l, 512)
    r3 = lambda a, n: a.reshape(bn, n, a.shape[-1])
    yb = _attn_sample(r3(q, t), r3(k_past, past), r3(k, t), r3(v_past, past), r3(v, t),
                      r3(zb, t), 8).reshape(m, B_WIDTH)
    out = _finish_layer(x, xn, ya, yb, yc, p, l, m, m)
    return out, (ckv, kpe), conv_new, a_vn


def kernel(x_prompt, x_sample, cache_ckv, cache_kpe, state_conv, norm_gain, w_in, a_v_gain, a_ws, a_bias,
           b_q_gain, b_w_qb, b_kv_gain, b_kr_gain, b_w_kvb, b_qn_gain, b_qr_gain, b_kn_gain, c_conv_w,
           w_pa, w_pb, w_pc, w_out):
    depth = w_in.shape[0]
    bn, s_len, _ = x_prompt.shape
    dbn, t, _ = x_sample.shape
    past = cache_ckv.shape[2]
    assert A_CHUNK % t == 0 and dbn * t == A_CHUNK and past % A_CHUNK == 0

    p = _prepare_params(norm_gain, w_in, a_v_gain, a_ws, a_bias, b_q_gain, b_w_qb, b_kv_gain, b_kr_gain,
                        b_w_kvb, b_qn_gain, b_qr_gain, b_kn_gain, c_conv_w, w_pa, w_pb, w_pc, w_out)
    tab_p = _rope_tables(jnp.arange(s_len, dtype=F32))
    tab_s = tuple(jnp.tile(tb, (dbn, 1)) for tb in _rope_tables(past + jnp.arange(t, dtype=F32)))

    hp = x_prompt.reshape(bn * s_len, D_MODEL)
    zero_state = jnp.zeros((1, bn, C_CONV - 1, C_WIDTH), F32)
    lat_p, conv_p = None, []
    for l in range(depth):
        hp, lat_p, cst = _prompt_layer(hp, p, l, tab_p, bn, s_len, lat_p, zero_state)
        conv_p.append(cst)

    m_s = dbn * t
    ws_blk = jnp.einsum('ab,lgts->lgatbs', jnp.eye(dbn, dtype=F32), a_ws[:, :, :t, :t]).reshape(
        depth, A_GROUPS, m_s, m_s)
    bias_s = jnp.swapaxes(jnp.tile(a_bias[:, :, :t], (1, 1, dbn)), 1, 2)
    ckv_c = cache_ckv.reshape(depth, dbn * past, B_KV_RANK)
    kpe_c = cache_kpe.reshape(depth, dbn * past, B_ROPE)
    hs = x_sample.reshape(m_s, D_MODEL)
    lat_s, conv_s, av_s = None, [], []
    for l in range(depth):
        hs, lat_s, cst, avn = _sample_layer(hs, p, l, tab_s, dbn, t, ws_blk, bias_s, state_conv,
                                            ckv_c, kpe_c, lat_s)
        conv_s.append(cst)
        av_s.append(avn.reshape(dbn, t, A_WIDTH))

    return (hp.reshape(bn, s_len, D_MODEL), hs.reshape(dbn, t, D_MODEL),
            lat_p[0].reshape(depth, bn, s_len, B_KV_RANK), lat_p[1].reshape(depth, bn, s_len, B_ROPE),
            jnp.stack(conv_p),
            lat_s[0].reshape(depth, dbn, t, B_KV_RANK), lat_s[1].reshape(depth, dbn, t, B_ROPE),
            jnp.stack(conv_s), jnp.stack(av_s))
```

```python
import functools
import math

import jax
import jax.numpy as jnp
from jax import lax
from jax.experimental import pallas as pl
from jax.experimental.pallas import tpu as pltpu

D_MODEL = 2048
CHUNK = 64
EPS = 1e-6
A_WIDTH = 1024
A_GROUPS = 4
A_GROUP_DIM = A_WIDTH // A_GROUPS
A_CHUNK = 128
B_HEADS = 16
B_Q_RANK = 512
B_KV_RANK = 512
B_NOPE = 128
B_ROPE = 64
B_VDIM = 128
B_QK = B_NOPE + B_ROPE
B_WIDTH = B_HEADS * B_VDIM
ATTN_SCALE = B_QK ** -0.5
ROPE_BASE = 10000.0
C_WIDTH = 1024
C_CONV = 3
N_BRANCH = 3

LANES = 128
HEAD_PAD = 2 * LANES
QK_WIDTH = B_HEADS * HEAD_PAD

R_C = 0
R_BZ = R_C + 4 * C_WIDTH
R_A = R_BZ + B_WIDTH
R_CQKV = R_A + 3 * A_WIDTH
R_KPE = R_CQKV + B_Q_RANK + B_KV_RANK
R_G = R_KPE + 4 * LANES
R_TOTAL = R_G + N_BRANCH * D_MODEL

VMEM_LIMIT = 56 * 1024 * 1024
BF16 = jnp.bfloat16
F32 = jnp.float32


def _dot(a, b):
    return jnp.dot(a, b, preferred_element_type=F32)


def _rms(x, gain, width=None):
    width = x.shape[-1] if width is None else width
    ms = jnp.sum(x * x, axis=-1, keepdims=True) * (1.0 / width)
    return x * lax.rsqrt(ms + EPS) * gain


def _silu(x):
    return x * (1.0 / (1.0 + jnp.exp(-x)))


def _sigmoid(x):
    return 1.0 / (1.0 + jnp.exp(-x))


def _params(sem):
    return pltpu.CompilerParams(dimension_semantics=sem, vmem_limit_bytes=VMEM_LIMIT)


def _layer_block(l, shape, blk=0):
    nd = len(shape)
    return pl.BlockSpec((None,) + tuple(shape), lambda *_: (l,) + (0,) * (nd - 1) + (blk,),
                        pipeline_mode=pl.Buffered(1))


def _w_in_block(l, width, offset):
    return _layer_block(l, (D_MODEL, width), offset // width)


def _rows(tm, width):
    return pl.BlockSpec((tm, width), lambda i: (i, 0))


def _branch_a_kernel(x_ref, ng_ref, w_ref, gain_ref, ws_ref, bias_ref, xn_ref, ya_ref, *vn_ref, tm):
    xn = _rms(x_ref[...], ng_ref[...]).astype(xn_ref.dtype)
    xn_ref[...] = xn
    row = lax.broadcasted_iota(jnp.int32, (A_CHUNK, A_CHUNK), 0)
    col = lax.broadcasted_iota(jnp.int32, (A_CHUNK, A_CHUNK), 1)
    for g in range(A_GROUPS):
        lo, hi = g * A_GROUP_DIM, (g + 1) * A_GROUP_DIM
        u = _dot(xn, w_ref[:, lo:hi])
        v = _dot(xn, w_ref[:, A_WIDTH + lo:A_WIDTH + hi])
        z = _dot(xn, w_ref[:, 2 * A_WIDTH + lo:2 * A_WIDTH + hi])
        vn = _rms(v, gain_ref[:, lo:hi])
        if vn_ref:
            vn_ref[0][:, lo:hi] = vn
        gate = u * _silu(z)
        ws = jnp.where(col <= row, ws_ref[g], 0.0).astype(BF16)
        bias = bias_ref[:, g:g + 1]
        vnb = vn.astype(BF16)
        for c in range(tm // A_CHUNK):
            r0, r1 = c * A_CHUNK, (c + 1) * A_CHUNK
            mixed = _dot(ws, vnb[r0:r1]) + bias
            ya_ref[r0:r1, lo:hi] = (gate[r0:r1] * mixed).astype(ya_ref.dtype)


def _branch_a(x, p, l, ws, bias_t, tm, emit_vn):
    m = x.shape[0]
    out_shape = [jax.ShapeDtypeStruct((m, D_MODEL), BF16), jax.ShapeDtypeStruct((m, A_WIDTH), BF16)]
    out_specs = [_rows(tm, D_MODEL), _rows(tm, A_WIDTH)]
    if emit_vn:
        out_shape.append(jax.ShapeDtypeStruct((m, A_WIDTH), F32))
        out_specs.append(_rows(tm, A_WIDTH))
    return pl.pallas_call(
        functools.partial(_branch_a_kernel, tm=tm),
        grid=(m // tm,),
        in_specs=[_rows(tm, D_MODEL), _layer_block(l, (1, D_MODEL)), _w_in_block(l, 3 * A_WIDTH, R_A),
                  _layer_block(l, (1, A_WIDTH)), _layer_block(l, ws.shape[1:]),
                  _layer_block(l, bias_t.shape[1:])],
        out_specs=out_specs,
        out_shape=out_shape,
        compiler_params=_params(("parallel",)),
        name="branch_a",
    )(x, p['norm'], p['w_in'], p['a_v_gain'], ws, bias_t)


def _branch_c_kernel(xn_ref, w_ref, cw_ref, state_ref, yc_ref, new_ref, pad_ref, *, seg, carry, cb):
    tm = xn_ref.shape[0]
    xn = xn_ref[...]
    stride = seg + 8
    if carry:
        first = pl.program_id(0) % carry == 0

        @pl.when(first)
        def _():
            pad_ref[6:8, :] = state_ref[0]
    for j in range(C_WIDTH // cb):
        cs = slice(j * cb, (j + 1) * cb)
        h, b, c, z = [_dot(xn, w_ref[:, k * C_WIDTH + j * cb:k * C_WIDTH + (j + 1) * cb])
                      for k in range(4)]
        hc = c * h
        gate = b * _silu(z)
        w0, w1, w2 = cw_ref[0:1, cs], cw_ref[1:2, cs], cw_ref[2:3, cs]
        for s in range(tm // seg):
            base = s * stride
            rows = slice(s * seg, (s + 1) * seg)
            if not carry:
                pad_ref[base + 6:base + 8, cs] = state_ref[s, :, cs]
            pad_ref[base + 8:base + 8 + seg, cs] = hc[rows]
            y = (w2 * hc[rows]
                 + w1 * pad_ref[base + 7:base + 7 + seg, cs]
                 + w0 * pad_ref[base + 6:base + 6 + seg, cs])
            yc_ref[rows, cs] = (gate[rows] * y).astype(yc_ref.dtype)
            new_ref[s, :, cs] = hc[(s + 1) * seg - 2:(s + 1) * seg]
        if carry:
            pad_ref[6:8, cs] = hc[tm - 2:tm]


def _branch_c(xn, p, l, state, state_layer, tm, seg, tiles_per_stream, cb):
    m = xn.shape[0]
    n_streams = state.shape[1]
    nseg = tm // seg
    blk = (nseg, C_CONV - 1, C_WIDTH)
    if tiles_per_stream:
        st_in = pl.BlockSpec((None,) + blk, lambda i: (state_layer, i // tiles_per_stream, 0, 0))
        st_out = pl.BlockSpec(blk, lambda i: (i // tiles_per_stream, 0, 0))
    else:
        st_in = pl.BlockSpec((None,) + blk, lambda i: (state_layer, i, 0, 0))
        st_out = pl.BlockSpec(blk, lambda i: (i, 0, 0))
    return pl.pallas_call(
        functools.partial(_branch_c_kernel, seg=seg, carry=tiles_per_stream, cb=cb),
        grid=(m // tm,),
        in_specs=[_rows(tm, D_MODEL), _w_in_block(l, 4 * C_WIDTH, R_C),
                  _layer_block(l, (C_CONV, C_WIDTH)), st_in],
        out_specs=[_rows(tm, C_WIDTH), st_out],
        out_shape=[jax.ShapeDtypeStruct((m, C_WIDTH), BF16),
                   jax.ShapeDtypeStruct((n_streams, C_CONV - 1, C_WIDTH), F32)],
        scratch_shapes=[pltpu.VMEM((nseg * (seg + 8), C_WIDTH), F32)],
        compiler_params=_params(("arbitrary",)),
        name="branch_c",
    )(xn, p['w_in'], p['conv_w'], state)


def _rope_padded(x, cos_ref, sin_lo_ref, sin_hi_ref):
    return (x * cos_ref[...]
            + pltpu.roll(x, LANES - B_ROPE // 2, 1) * sin_lo_ref[...]
            + pltpu.roll(x, B_ROPE // 2, 1) * sin_hi_ref[...])


def _expand_kv(ckv_n, kpe_pad, wkvb_ref, kng_ref, k_ref, v_ref):
    kv = _dot(ckv_n.astype(BF16), wkvb_ref[...])
    v_ref[...] = kv[:, B_HEADS * B_NOPE:].astype(v_ref.dtype)
    kpe_b = kpe_pad.astype(k_ref.dtype)
    kn = [_rms(kv[:, hd * B_NOPE:(hd + 1) * B_NOPE], kng_ref[...]) for hd in range(B_HEADS)]
    for hd in range(B_HEADS):
        k_ref[:, hd * HEAD_PAD:hd * HEAD_PAD + LANES] = kn[hd].astype(k_ref.dtype)
        k_ref[:, hd * HEAD_PAD + LANES:(hd + 1) * HEAD_PAD] = kpe_b


def _branch_b_kernel(xn_ref, wcqkv_ref, wkpe_ref, wbz_ref, wqb_ref, wkvb_ref, qg_ref, kvg_ref, krg_ref,
                     qng_ref, qrg_ref, kng_ref, cos_ref, sin_lo_ref, sin_hi_ref, *rest):
    q_ref, k_ref, v_ref, zb_ref, ckv_ref, kpe_ref = rest[-6:]
    xn = xn_ref[...]
    cq = _rms(_dot(xn, wcqkv_ref[:, 0:B_Q_RANK]), qg_ref[...])
    q = _dot(cq.astype(BF16), wqb_ref[...])
    heads = range(B_HEADS)
    qa = [_rms(q[:, hd * HEAD_PAD:hd * HEAD_PAD + LANES], qng_ref[...]) for hd in heads]
    qb = [_rms(q[:, hd * HEAD_PAD + LANES:(hd + 1) * HEAD_PAD], qrg_ref[...], B_ROPE) for hd in heads]
    qb = [_rope_padded(qb[hd], cos_ref, sin_lo_ref, sin_hi_ref) for hd in heads]
    for hd in heads:
        q_ref[:, hd * HEAD_PAD:hd * HEAD_PAD + LANES] = qa[hd].astype(q_ref.dtype)
        q_ref[:, hd * HEAD_PAD + LANES:(hd + 1) * HEAD_PAD] = qb[hd].astype(q_ref.dtype)
    ckv_n = _rms(_dot(xn, wcqkv_ref[:, B_Q_RANK:B_Q_RANK + B_KV_RANK]), kvg_ref[...])
    ckv_ref[...] = ckv_n
    kpe_pad = _rope_padded(_rms(_dot(xn, wkpe_ref[...]), krg_ref[...], B_ROPE),
                           cos_ref, sin_lo_ref, sin_hi_ref)
    kpe_ref[...] = kpe_pad[:, 0:B_ROPE]
    _expand_kv(ckv_n, kpe_pad, wkvb_ref, kng_ref, k_ref, v_ref)
    zb_ref[...] = _silu(_dot(xn, wbz_ref[...]))


def _branch_b(xn, p, l, tables, tm, table_tiles, latents):
    m = xn.shape[0]
    depth = p['w_in'].shape[0]
    tab = pl.BlockSpec((tm, LANES), lambda i: (i % table_tiles, 0))
    gains = [p[n] for n in ('b_q_gain', 'b_kv_gain', 'b_kr_gain', 'b_qn_gain', 'b_qr_gain', 'b_kn_gain')]
    acc = [] if latents is None else list(latents)
    n_in = 15
    return pl.pallas_call(
        _branch_b_kernel,
        grid=(m // tm,),
        in_specs=[_rows(tm, D_MODEL),
                  _w_in_block(l, B_Q_RANK + B_KV_RANK, R_CQKV), _w_in_block(l, LANES, R_KPE),
                  _w_in_block(l, B_WIDTH, R_BZ),
                  _layer_block(l, (B_Q_RANK, QK_WIDTH)), _layer_block(l, (B_KV_RANK, 2 * B_WIDTH))]
                 + [_layer_block(l, g.shape[1:]) for g in gains] + [tab, tab, tab]
                 + [pl.BlockSpec(memory_space=pl.ANY)] * len(acc),
        out_specs=[_rows(tm, QK_WIDTH), _rows(tm, QK_WIDTH), _rows(tm, B_WIDTH), _rows(tm, B_WIDTH),
                   pl.BlockSpec((None, tm, B_KV_RANK), lambda i: (l, i, 0)),
                   pl.BlockSpec((None, tm, B_ROPE), lambda i: (l, i, 0))],
        out_shape=[jax.ShapeDtypeStruct((m, QK_WIDTH), BF16),
                   jax.ShapeDtypeStruct((m, QK_WIDTH), BF16),
                   jax.ShapeDtypeStruct((m, B_WIDTH), BF16),
                   jax.ShapeDtypeStruct((m, B_WIDTH), F32),
                   jax.ShapeDtypeStruct((depth, m, B_KV_RANK), F32),
                   jax.ShapeDtypeStruct((depth, m, B_ROPE), F32)],
        input_output_aliases={n_in + j: 4 + j for j in range(len(acc))},
        compiler_params=_params(("parallel",)),
        name="branch_b",
    )(xn, p['w_in'], p['w_in'], p['w_in'], p['wqb'], p['wkvb'], *gains, *tables, *acc)


def _qk(q, k):
    return lax.dot_general(q, k, (((1,), (1,)), ((), ())), preferred_element_type=F32)


EXP2_SCALE = ATTN_SCALE * math.log2(math.e)


def _attn_prompt_kernel(q_ref, k_ref, v_ref, zb_ref, o_ref, vext_ref, *, tq, heads):
    s_len = q_ref.shape[0]
    rowc = lax.broadcasted_iota(jnp.int32, (tq, tq), 0) // CHUNK
    colc = lax.broadcasted_iota(jnp.int32, (tq, tq), 1) // CHUNK
    visible = colc <= rowc
    for hh in range(heads):
        vext_ref[hh, :, 0:B_VDIM] = v_ref[:, hh * B_VDIM:(hh + 1) * B_VDIM]
        vext_ref[hh, :, B_VDIM:] = jnp.ones((s_len, LANES), vext_ref.dtype)
    hs = range(heads)
    qs = [slice(hh * HEAD_PAD, (hh + 1) * HEAD_PAD) for hh in hs]
    vs = [slice(hh * B_VDIM, (hh + 1) * B_VDIM) for hh in hs]

    def scores(qi):
        r0 = qi * tq
        q_t = [q_ref[r0:r0 + tq, qs[hh]] for hh in hs]
        s_diag = [jnp.where(visible, _qk(q_t[hh], k_ref[r0:r0 + tq, qs[hh]]), -jnp.inf) for hh in hs]
        mx = [jnp.max(s_diag[hh], axis=-1, keepdims=True) for hh in hs]
        s_prev = None
        if qi:
            s_prev = [_qk(q_t[hh], k_ref[0:r0, qs[hh]]) for hh in hs]
            mx = [jnp.maximum(mx[hh], jnp.max(s_prev[hh], axis=-1, keepdims=True)) for hh in hs]
        return s_diag, s_prev, mx

    def weighted_values(qi, s_diag, s_prev, mx):
        r0 = qi * tq
        mc = [mx[hh] * EXP2_SCALE for hh in hs]
        p_diag = [jnp.exp2(s_diag[hh] * EXP2_SCALE - mc[hh]).astype(BF16) for hh in hs]
        acc = [_dot(p_diag[hh], vext_ref[hh, r0:r0 + tq, :]) for hh in hs]
        if qi:
            p_prev = [jnp.exp2(s_prev[hh] * EXP2_SCALE - mc[hh]).astype(BF16) for hh in hs]
            acc = [acc[hh] + _dot(p_prev[hh], vext_ref[hh, 0:r0, :]) for hh in hs]
        for hh in hs:
            o = acc[hh][:, 0:B_VDIM] / acc[hh][:, B_VDIM:]
            o_ref[r0:r0 + tq, vs[hh]] = (o * zb_ref[r0:r0 + tq, vs[hh]]).astype(o_ref.dtype)

    for qi in range(s_len // tq):
        weighted_values(qi, *scores(qi))


def _attn_prompt(q, k, v, zb, tq, heads):
    bn, s_len, _ = q.shape
    blk = lambda w: pl.BlockSpec((None, s_len, heads * w), lambda b, h: (b, 0, h))
    return pl.pallas_call(
        functools.partial(_attn_prompt_kernel, tq=tq, heads=heads),
        grid=(bn, B_HEADS // heads),
        in_specs=[blk(HEAD_PAD), blk(HEAD_PAD), blk(B_VDIM), blk(B_VDIM)],
        out_specs=blk(B_VDIM),
        out_shape=jax.ShapeDtypeStruct((bn, s_len, B_WIDTH), BF16),
        scratch_shapes=[pltpu.VMEM((heads, s_len, B_VDIM + LANES), BF16)],
        compiler_params=_params(("parallel", "parallel")),
        name="attn_prompt",
    )(q, k, v, zb)


SAMPLE_HEAD_GROUP = 4


def _attn_sample_kernel(ckv_ref, kpe_ref, ckvn_ref, kpen_ref, qbd_ref, qpe_ref, zb_ref, wkvb_ref, kng_ref,
                        o_ref, kn_ref):
    past, t = ckv_ref.shape[0], ckvn_ref.shape[0]
    ckv = ckv_ref[...].astype(BF16)
    ckvn = ckvn_ref[...].astype(BF16)
    gw = SAMPLE_HEAD_GROUP * B_NOPE
    for g in range(B_HEADS // SAMPLE_HEAD_GROUP):
        kx = _dot(ckv, wkvb_ref[:, g * gw:(g + 1) * gw])
        kxn = _dot(ckvn, wkvb_ref[:, g * gw:(g + 1) * gw])
        for hd in range(SAMPLE_HEAD_GROUP):
            src = slice(hd * B_NOPE, (hd + 1) * B_NOPE)
            dst = slice(g * gw + hd * B_NOPE, g * gw + (hd + 1) * B_NOPE)
            kn_ref[0:past, dst] = _rms(kx[:, src], kng_ref[...]).astype(kn_ref.dtype)
            kn_ref[past:past + t, dst] = _rms(kxn[:, src], kng_ref[...]).astype(kn_ref.dtype)
    qbd, qpe = qbd_ref[...], qpe_ref[...]
    sp = (_qk(qbd, kn_ref[0:past, :]) + _qk(qpe, kpe_ref[...].astype(BF16))) * ATTN_SCALE
    sn = (_qk(qbd, kn_ref[past:past + t, :]) + _qk(qpe, kpen_ref[...].astype(BF16))) * ATTN_SCALE
    mx = jnp.maximum(jnp.max(sp, axis=-1, keepdims=True), jnp.max(sn, axis=-1, keepdims=True))
    pp = jnp.exp(sp - mx)
    pn = jnp.exp(sn - mx)
    inv = 1.0 / (jnp.sum(pp, axis=-1, keepdims=True) + jnp.sum(pn, axis=-1, keepdims=True))
    u = _dot((pp * inv).astype(BF16), ckv) + _dot((pn * inv).astype(BF16), ckvn)
    o_all = _dot(u.astype(BF16), wkvb_ref[:, B_HEADS * B_NOPE:])
    for hd in range(B_HEADS):
        vs = slice(hd * B_VDIM, (hd + 1) * B_VDIM)
        o_ref[:, vs] = (o_all[hd * t:(hd + 1) * t, vs] * zb_ref[:, vs]).astype(o_ref.dtype)


def _attn_sample(q, zb, cache_ckv, cache_kpe, latents, p, l, bn, t):
    past = cache_ckv.shape[1] // bn
    q4 = q.reshape(bn, t, B_HEADS, HEAD_PAD)
    qbd = jnp.einsum('bihd,hg->bhigd', q4[..., :B_NOPE], jnp.eye(B_HEADS, dtype=q.dtype)).reshape(
        bn, B_HEADS * t, B_HEADS * B_NOPE)
    qpe = jnp.transpose(q4[..., B_NOPE:B_QK], (0, 2, 1, 3)).reshape(bn, B_HEADS * t, B_ROPE)
    lrows = lambda n, w: pl.BlockSpec((None, n, w), lambda b: (l, b, 0))
    per_stream = lambda r, c: pl.BlockSpec((None, r, c), lambda b: (b, 0, 0))
    return pl.pallas_call(
        _attn_sample_kernel,
        grid=(bn,),
        in_specs=[lrows(past, B_KV_RANK), lrows(past, B_ROPE), lrows(t, B_KV_RANK), lrows(t, B_ROPE),
                  per_stream(B_HEADS * t, B_HEADS * B_NOPE), per_stream(B_HEADS * t, B_ROPE),
                  _rows(t, B_WIDTH), _layer_block(l, (B_KV_RANK, 2 * B_WIDTH)),
                  _layer_block(l, (1, B_NOPE))],
        out_specs=_rows(t, B_WIDTH),
        out_shape=jax.ShapeDtypeStruct((bn * t, B_WIDTH), BF16),
        scratch_shapes=[pltpu.VMEM((past + t, B_HEADS * B_NOPE), BF16)],
        compiler_params=_params(("parallel",)),
        name="attn_sample",
    )(cache_ckv, cache_kpe, latents[0], latents[1], qbd, qpe, zb, p['wkvb'], p['b_kn_gain'])


def _merge_kernel(xn_ref, ya_ref, yb_ref, yc_ref, wg0_ref, wg1_ref, wg2_ref,
                  wpa_ref, wpb_ref, wpc_ref, h_ref):
    xn = xn_ref[...]
    h = (_sigmoid(_dot(xn, wg0_ref[...])) * _dot(ya_ref[...], wpa_ref[...])
         + _sigmoid(_dot(xn, wg1_ref[...])) * _dot(yb_ref[...], wpb_ref[...])
         + _sigmoid(_dot(xn, wg2_ref[...])) * _dot(yc_ref[...], wpc_ref[...]))
    h_ref[...] = h.astype(h_ref.dtype)


def _merge(xn, ya, yb, yc, p, l, tm, tn):
    m = xn.shape[0]
    nj = D_MODEL // tn
    row = lambda w: pl.BlockSpec((tm, w), lambda i, j: (i, 0))
    colw = lambda k, off: pl.BlockSpec((None, k, tn), lambda i, j: (l, 0, j + off))
    g0 = R_G // tn
    return pl.pallas_call(
        _merge_kernel,
        grid=(m // tm, nj),
        in_specs=[row(D_MODEL), row(A_WIDTH), row(B_WIDTH), row(C_WIDTH),
                  colw(D_MODEL, g0), colw(D_MODEL, g0 + nj), colw(D_MODEL, g0 + 2 * nj),
                  colw(A_WIDTH, 0), colw(B_WIDTH, 0), colw(C_WIDTH, 0)],
        out_specs=pl.BlockSpec((tm, tn), lambda i, j: (i, j)),
        out_shape=jax.ShapeDtypeStruct((m, D_MODEL), BF16),
        compiler_params=_params(("parallel", "parallel")),
        name="merge",
    )(xn, ya, yb, yc, p['w_in'], p['w_in'], p['w_in'], p['w_pa'], p['w_pb'], p['w_pc'])


def _out_kernel(x_ref, h_ref, w_ref, o_ref):
    o_ref[...] = x_ref[...] + _dot(h_ref[...], w_ref[...])


def _out_proj(x, h, p, l, tm):
    m = x.shape[0]
    return pl.pallas_call(
        _out_kernel,
        grid=(m // tm,),
        in_specs=[_rows(tm, D_MODEL), _rows(tm, D_MODEL), _layer_block(l, (D_MODEL, D_MODEL))],
        out_specs=_rows(tm, D_MODEL),
        out_shape=jax.ShapeDtypeStruct((m, D_MODEL), F32),
        compiler_params=_params(("parallel",)),
        name="out_proj",
    )(x, h, p['w_out'])


S_A = 0
S_CQKV = S_A + 3 * A_WIDTH
S_KPE = S_CQKV + B_Q_RANK + B_KV_RANK
S_BZ = S_KPE + B_ROPE
S_C = S_BZ + B_WIDTH
S_G = S_C + 4 * C_WIDTH
S_TOTAL = S_G + N_BRANCH * D_MODEL


RL_ROWS = 64
RL_PIECES = 8
RL_COLS = RL_ROWS * RL_PIECES
RL_SEGMENTS = ((R_C, S_C - R_C), (R_BZ, S_BZ - R_BZ), (R_A, S_A - R_A), (R_CQKV, S_CQKV - R_CQKV),
               (R_KPE, S_KPE - R_KPE), (R_G, S_G - R_G))
RL_KPE_BLOCK = R_KPE // RL_COLS


def _relayout_source_piece(j, k):
    delta = RL_SEGMENTS[0][1] // RL_ROWS
    for start, d in RL_SEGMENTS[1:]:
        delta = jnp.where(j >= start // RL_COLS, d // RL_ROWS, delta)
    return RL_PIECES * j + delta + k


def _relayout_kernel(*refs):
    o_ref = refs[-1]
    x = jnp.concatenate([r[...] for r in refs[:-1]], axis=0)
    row = lax.broadcasted_iota(jnp.int32, (RL_COLS, 1), 0)
    keep = jnp.logical_or(pl.program_id(1) != RL_KPE_BLOCK, row < B_ROPE)
    o_ref[...] = jnp.where(keep, x, 0.0).T.astype(o_ref.dtype)


def _relayout_w_in(w_in):
    depth = w_in.shape[0]
    assert w_in.shape[2] == S_TOTAL and R_TOTAL % RL_COLS == 0
    w_t = jnp.swapaxes(w_in, 1, 2)
    piece = lambda k: pl.BlockSpec((None, RL_ROWS, D_MODEL),
                                   lambda l, j: (l, _relayout_source_piece(j, k), 0))
    return pl.pallas_call(
        _relayout_kernel,
        grid=(depth, R_TOTAL // RL_COLS),
        in_specs=[piece(k) for k in range(RL_PIECES)],
        out_specs=pl.BlockSpec((None, D_MODEL, RL_COLS), lambda l, j: (l, 0, j)),
        out_shape=jax.ShapeDtypeStruct((depth, D_MODEL, R_TOTAL), BF16),
        compiler_params=_params(("parallel", "parallel")),
        name="relayout_w_in",
    )(*([w_t] * RL_PIECES))


def _rope_tables(pos):
    half = B_ROPE // 2
    inv = ROPE_BASE ** (-jnp.arange(half, dtype=F32) / half)
    ang = pos[:, None] * inv[None, :]
    c, s = jnp.cos(ang), jnp.sin(ang)
    z = jnp.zeros_like(c)
    return (jnp.concatenate([c, c, z, z], axis=1),
            jnp.concatenate([-s, z, z, z], axis=1),
            jnp.concatenate([z, s, z, z], axis=1))


def _prepare_params(norm_gain, w_in, a_v_gain, a_ws, a_bias, b_q_gain, b_w_qb, b_kv_gain, b_kr_gain,
                    b_w_kvb, b_qn_gain, b_qr_gain, b_kn_gain, c_conv_w, w_pa, w_pb, w_pc, w_out):
    depth = w_in.shape[0]
    w_r = _relayout_w_in(w_in)
    wqb = jnp.pad(b_w_qb.reshape(depth, B_Q_RANK, B_HEADS, B_QK),
                  ((0, 0), (0, 0), (0, 0), (0, HEAD_PAD - B_QK))).reshape(depth, B_Q_RANK, QK_WIDTH)
    wkv = b_w_kvb.reshape(depth, B_KV_RANK, B_HEADS, B_NOPE + B_VDIM)
    wkvb = jnp.concatenate([wkv[..., :B_NOPE].reshape(depth, B_KV_RANK, B_HEADS * B_NOPE),
                            wkv[..., B_NOPE:].reshape(depth, B_KV_RANK, B_WIDTH)], axis=2)
    row = lambda g: g[:, None, :]
    pad_rope = lambda g: jnp.pad(g, ((0, 0), (0, LANES - B_ROPE)))[:, None, :]
    return dict(
        norm=row(norm_gain), w_in=w_r, a_v_gain=row(a_v_gain), a_ws=a_ws, a_bias=a_bias,
        wqb=wqb.astype(BF16), wkvb=wkvb.astype(BF16),
        b_q_gain=row(b_q_gain), b_kv_gain=row(b_kv_gain), b_kr_gain=pad_rope(b_kr_gain),
        b_qn_gain=row(b_qn_gain), b_qr_gain=pad_rope(b_qr_gain), b_kn_gain=row(b_kn_gain),
        conv_w=c_conv_w,
        w_pa=w_pa.astype(BF16), w_pb=w_pb.astype(BF16), w_pc=w_pc.astype(BF16),
        w_out=w_out.astype(BF16))


def _finish_layer(x, xn, ya, yb, yc, p, l, tm_merge, tm_out):
    h = _merge(xn, ya, yb, yc, p, l, tm_merge, 256)
    return _out_proj(x, h, p, l, tm_out)


def _prompt_layer(x, p, l, tables, bn, s_len, latents, zero_state):
    tm = 1024
    xn, ya = _branch_a(x, p, l, p['a_ws'], jnp.swapaxes(p['a_bias'], 1, 2), tm, False)
    tm_c = 1024
    yc, conv_new = _branch_c(xn, p, l, zero_state, 0, tm_c, tm_c, s_len // tm_c, 512)
    tm_b = 256
    q, k, v, zb, ckv, kpe = _branch_b(xn, p, l, tables, tm_b, s_len // tm_b, latents)
    r3 = lambda a: a.reshape(bn, s_len, a.shape[-1])
    yb = _attn_prompt(r3(q), r3(k), r3(v), r3(zb), 256, 4).reshape(bn * s_len, B_WIDTH)
    out = _finish_layer(x, xn, ya, yb, yc, p, l, 1024, tm)
    return out, (ckv, kpe), conv_new


def _sample_layer(x, p, l, tables, bn, t, ws_blk, bias_t, state_conv, cache_ckv, cache_kpe, latents):
    m = bn * t
    xn, ya, a_vn = _branch_a(x, p, l, ws_blk, bias_t, m, True)
    yc, conv_new = _branch_c(xn, p, l, state_conv, l, m, t, 0, C_WIDTH)
    q, k, v, zb, ckv, kpe = _branch_b(xn, p, l, tables, m, 1, latents)
    yb = _attn_sample(q, zb, cache_ckv, cache_kpe, (ckv, kpe), p, l, bn, t)
    out = _finish_layer(x, xn, ya, yb, yc, p, l, m, m)
    return out, (ckv, kpe), conv_new, a_vn


def kernel(x_prompt, x_sample, cache_ckv, cache_kpe, state_conv, norm_gain, w_in, a_v_gain, a_ws, a_bias,
           b_q_gain, b_w_qb, b_kv_gain, b_kr_gain, b_w_kvb, b_qn_gain, b_qr_gain, b_kn_gain, c_conv_w,
           w_pa, w_pb, w_pc, w_out):
    depth = w_in.shape[0]
    bn, s_len, _ = x_prompt.shape
    dbn, t, _ = x_sample.shape
    past = cache_ckv.shape[2]
    assert A_CHUNK % t == 0 and dbn * t == A_CHUNK and past % A_CHUNK == 0

    p = _prepare_params(norm_gain, w_in, a_v_gain, a_ws, a_bias, b_q_gain, b_w_qb, b_kv_gain, b_kr_gain,
                        b_w_kvb, b_qn_gain, b_qr_gain, b_kn_gain, c_conv_w, w_pa, w_pb, w_pc, w_out)
    tab_p = _rope_tables(jnp.arange(s_len, dtype=F32))
    tab_s = tuple(jnp.tile(tb, (dbn, 1)) for tb in _rope_tables(past + jnp.arange(t, dtype=F32)))

    hp = x_prompt.reshape(bn * s_len, D_MODEL)
    zero_state = jnp.zeros((1, bn, C_CONV - 1, C_WIDTH), F32)
    lat_p, conv_p = None, []
    for l in range(depth):
        hp, lat_p, cst = _prompt_layer(hp, p, l, tab_p, bn, s_len, lat_p, zero_state)
        conv_p.append(cst)

    m_s = dbn * t
    ws_blk = jnp.einsum('ab,lgts->lgatbs', jnp.eye(dbn, dtype=F32), a_ws[:, :, :t, :t]).reshape(
        depth, A_GROUPS, m_s, m_s)
    bias_s = jnp.swapaxes(jnp.tile(a_bias[:, :, :t], (1, 1, dbn)), 1, 2)
    ckv_c = cache_ckv.reshape(depth, dbn * past, B_KV_RANK)
    kpe_c = cache_kpe.reshape(depth, dbn * past, B_ROPE)
    hs = x_sample.reshape(m_s, D_MODEL)
    lat_s, conv_s, av_s = None, [], []
    for l in range(depth):
        hs, lat_s, cst, avn = _sample_layer(hs, p, l, tab_s, dbn, t, ws_blk, bias_s, state_conv,
                                            ckv_c, kpe_c, lat_s)
        conv_s.append(cst)
        av_s.append(avn.reshape(dbn, t, A_WIDTH))

    return (hp.reshape(bn, s_len, D_MODEL), hs.reshape(dbn, t, D_MODEL),
            lat_p[0].reshape(depth, bn, s_len, B_KV_RANK), lat_p[1].reshape(depth, bn, s_len, B_ROPE),
            jnp.stack(conv_p),
            lat_s[0].reshape(depth, dbn, t, B_KV_RANK), lat_s[1].reshape(depth, dbn, t, B_ROPE),
            jnp.stack(conv_s), jnp.stack(av_s))
```

```python
import functools
import math

import jax
import jax.numpy as jnp
from jax import lax
from jax.experimental import pallas as pl
from jax.experimental.pallas import tpu as pltpu

D_MODEL = 2048
CHUNK = 64
EPS = 1e-6
A_WIDTH = 1024
A_GROUPS = 4
A_GROUP_DIM = A_WIDTH // A_GROUPS
A_CHUNK = 128
B_HEADS = 16
B_Q_RANK = 512
B_KV_RANK = 512
B_NOPE = 128
B_ROPE = 64
B_VDIM = 128
B_QK = B_NOPE + B_ROPE
B_WIDTH = B_HEADS * B_VDIM
ATTN_SCALE = B_QK ** -0.5
ROPE_BASE = 10000.0
C_WIDTH = 1024
C_CONV = 3
N_BRANCH = 3

LANES = 128
HEAD_PAD = 2 * LANES
QK_WIDTH = B_HEADS * HEAD_PAD

R_C = 0
R_BZ = R_C + 4 * C_WIDTH
R_A = R_BZ + B_WIDTH
R_CQKV = R_A + 3 * A_WIDTH
R_KPE = R_CQKV + B_Q_RANK + B_KV_RANK
R_G = R_KPE + 4 * LANES
R_TOTAL = R_G + N_BRANCH * D_MODEL

VMEM_LIMIT = 56 * 1024 * 1024
BF16 = jnp.bfloat16
F32 = jnp.float32


def _dot(a, b):
    return jnp.dot(a, b, preferred_element_type=F32)


def _rms(x, gain, width=None):
    width = x.shape[-1] if width is None else width
    ms = jnp.sum(x * x, axis=-1, keepdims=True) * (1.0 / width)
    return x * lax.rsqrt(ms + EPS) * gain


def _silu(x):
    return x * (1.0 / (1.0 + jnp.exp(-x)))


def _sigmoid(x):
    return 1.0 / (1.0 + jnp.exp(-x))


def _params(sem):
    return pltpu.CompilerParams(dimension_semantics=sem, vmem_limit_bytes=VMEM_LIMIT)


def _layer_block(l, shape, blk=0):
    nd = len(shape)
    return pl.BlockSpec((None,) + tuple(shape), lambda *_: (l,) + (0,) * (nd - 1) + (blk,),
                        pipeline_mode=pl.Buffered(1))


def _w_in_block(l, width, offset):
    return _layer_block(l, (D_MODEL, width), offset // width)


def _rows(tm, width):
    return pl.BlockSpec((tm, width), lambda i: (i, 0))


def _branch_a_kernel(x_ref, ng_ref, w_ref, gain_ref, ws_ref, bias_ref, xn_ref, ya_ref, *vn_ref, tm):
    xn = _rms(x_ref[...], ng_ref[...]).astype(xn_ref.dtype)
    xn_ref[...] = xn
    row = lax.broadcasted_iota(jnp.int32, (A_CHUNK, A_CHUNK), 0)
    col = lax.broadcasted_iota(jnp.int32, (A_CHUNK, A_CHUNK), 1)
    for g in range(A_GROUPS):
        lo, hi = g * A_GROUP_DIM, (g + 1) * A_GROUP_DIM
        u = _dot(xn, w_ref[:, lo:hi])
        v = _dot(xn, w_ref[:, A_WIDTH + lo:A_WIDTH + hi])
        z = _dot(xn, w_ref[:, 2 * A_WIDTH + lo:2 * A_WIDTH + hi])
        vn = _rms(v, gain_ref[:, lo:hi])
        if vn_ref:
            vn_ref[0][:, lo:hi] = vn
        gate = u * _silu(z)
        ws = jnp.where(col <= row, ws_ref[g], 0.0).astype(BF16)
        bias = bias_ref[:, g:g + 1]
        vnb = vn.astype(BF16)
        for c in range(tm // A_CHUNK):
            r0, r1 = c * A_CHUNK, (c + 1) * A_CHUNK
            mixed = _dot(ws, vnb[r0:r1]) + bias
            ya_ref[r0:r1, lo:hi] = (gate[r0:r1] * mixed).astype(ya_ref.dtype)


def _branch_a(x, p, l, ws, bias_t, tm, emit_vn):
    m = x.shape[0]
    out_shape = [jax.ShapeDtypeStruct((m, D_MODEL), BF16), jax.ShapeDtypeStruct((m, A_WIDTH), BF16)]
    out_specs = [_rows(tm, D_MODEL), _rows(tm, A_WIDTH)]
    if emit_vn:
        out_shape.append(jax.ShapeDtypeStruct((m, A_WIDTH), F32))
        out_specs.append(_rows(tm, A_WIDTH))
    return pl.pallas_call(
        functools.partial(_branch_a_kernel, tm=tm),
        grid=(m // tm,),
        in_specs=[_rows(tm, D_MODEL), _layer_block(l, (1, D_MODEL)), _w_in_block(l, 3 * A_WIDTH, R_A),
                  _layer_block(l, (1, A_WIDTH)), _layer_block(l, ws.shape[1:]),
                  _layer_block(l, bias_t.shape[1:])],
        out_specs=out_specs,
        out_shape=out_shape,
        compiler_params=_params(("parallel",)),
        name="branch_a",
    )(x, p['norm'], p['w_in'], p['a_v_gain'], ws, bias_t)


def _branch_c_kernel(xn_ref, w_ref, cw_ref, state_ref, yc_ref, new_ref, pad_ref, *, seg, carry, cb):
    tm = xn_ref.shape[0]
    xn = xn_ref[...]
    stride = seg + 8
    if carry:
        first = pl.program_id(0) % carry == 0

        @pl.when(first)
        def _():
            pad_ref[6:8, :] = state_ref[0]
    for j in range(C_WIDTH // cb):
        cs = slice(j * cb, (j + 1) * cb)
        h, b, c, z = [_dot(xn, w_ref[:, k * C_WIDTH + j * cb:k * C_WIDTH + (j + 1) * cb])
                      for k in range(4)]
        hc = c * h
        gate = b * _silu(z)
        w0, w1, w2 = cw_ref[0:1, cs], cw_ref[1:2, cs], cw_ref[2:3, cs]
        for s in range(tm // seg):
            base = s * stride
            rows = slice(s * seg, (s + 1) * seg)
            if not carry:
                pad_ref[base + 6:base + 8, cs] = state_ref[s, :, cs]
            pad_ref[base + 8:base + 8 + seg, cs] = hc[rows]
            y = (w2 * hc[rows]
                 + w1 * pad_ref[base + 7:base + 7 + seg, cs]
                 + w0 * pad_ref[base + 6:base + 6 + seg, cs])
            yc_ref[rows, cs] = (gate[rows] * y).astype(yc_ref.dtype)
            new_ref[s, :, cs] = hc[(s + 1) * seg - 2:(s + 1) * seg]
        if carry:
            pad_ref[6:8, cs] = hc[tm - 2:tm]


def _branch_c(xn, p, l, state, state_layer, tm, seg, tiles_per_stream, cb):
    m = xn.shape[0]
    n_streams = state.shape[1]
    nseg = tm // seg
    blk = (nseg, C_CONV - 1, C_WIDTH)
    if tiles_per_stream:
        st_in = pl.BlockSpec((None,) + blk, lambda i: (state_layer, i // tiles_per_stream, 0, 0))
        st_out = pl.BlockSpec(blk, lambda i: (i // tiles_per_stream, 0, 0))
    else:
        st_in = pl.BlockSpec((None,) + blk, lambda i: (state_layer, i, 0, 0))
        st_out = pl.BlockSpec(blk, lambda i: (i, 0, 0))
    return pl.pallas_call(
        functools.partial(_branch_c_kernel, seg=seg, carry=tiles_per_stream, cb=cb),
        grid=(m // tm,),
        in_specs=[_rows(tm, D_MODEL), _w_in_block(l, 4 * C_WIDTH, R_C),
                  _layer_block(l, (C_CONV, C_WIDTH)), st_in],
        out_specs=[_rows(tm, C_WIDTH), st_out],
        out_shape=[jax.ShapeDtypeStruct((m, C_WIDTH), BF16),
                   jax.ShapeDtypeStruct((n_streams, C_CONV - 1, C_WIDTH), F32)],
        scratch_shapes=[pltpu.VMEM((nseg * (seg + 8), C_WIDTH), F32)],
        compiler_params=_params(("arbitrary",)),
        name="branch_c",
    )(xn, p['w_in'], p['conv_w'], state)


def _rope_padded(x, cos_ref, sin_lo_ref, sin_hi_ref):
    return (x * cos_ref[...]
            + pltpu.roll(x, LANES - B_ROPE // 2, 1) * sin_lo_ref[...]
            + pltpu.roll(x, B_ROPE // 2, 1) * sin_hi_ref[...])


def _expand_kv(ckv_n, kpe_pad, wkvb_ref, kng_ref, k_ref, v_ref):
    kv = _dot(ckv_n.astype(BF16), wkvb_ref[...])
    v_ref[...] = kv[:, B_HEADS * B_NOPE:].astype(v_ref.dtype)
    kpe_b = kpe_pad.astype(k_ref.dtype)
    kn = [_rms(kv[:, hd * B_NOPE:(hd + 1) * B_NOPE], kng_ref[...]) for hd in range(B_HEADS)]
    for hd in range(B_HEADS):
        k_ref[:, hd * HEAD_PAD:hd * HEAD_PAD + LANES] = kn[hd].astype(k_ref.dtype)
        k_ref[:, hd * HEAD_PAD + LANES:(hd + 1) * HEAD_PAD] = kpe_b


def _branch_b_kernel(xn_ref, wcqkv_ref, wkpe_ref, wbz_ref, wqb_ref, wkvb_ref, qg_ref, kvg_ref, krg_ref,
                     qng_ref, qrg_ref, kng_ref, cos_ref, sin_lo_ref, sin_hi_ref, *rest):
    q_ref, k_ref, v_ref, zb_ref, ckv_ref, kpe_ref = rest[-6:]
    xn = xn_ref[...]
    half = B_WIDTH // 2
    cq = _rms(_dot(xn, wcqkv_ref[:, 0:B_Q_RANK]), qg_ref[...])
    zb_ref[:, 0:half] = _silu(_dot(xn, wbz_ref[:, 0:half]))
    q = _dot(cq.astype(BF16), wqb_ref[...])
    heads = range(B_HEADS)
    qa = [_rms(q[:, hd * HEAD_PAD:hd * HEAD_PAD + LANES], qng_ref[...]) for hd in heads]
    qb = [_rms(q[:, hd * HEAD_PAD + LANES:(hd + 1) * HEAD_PAD], qrg_ref[...], B_ROPE) for hd in heads]
    qb = [_rope_padded(qb[hd], cos_ref, sin_lo_ref, sin_hi_ref) for hd in heads]
    for hd in heads:
        q_ref[:, hd * HEAD_PAD:hd * HEAD_PAD + LANES] = qa[hd].astype(q_ref.dtype)
        q_ref[:, hd * HEAD_PAD + LANES:(hd + 1) * HEAD_PAD] = qb[hd].astype(q_ref.dtype)
    ckv_n = _rms(_dot(xn, wcqkv_ref[:, B_Q_RANK:B_Q_RANK + B_KV_RANK]), kvg_ref[...])
    ckv_ref[...] = ckv_n
    kpe_pad = _rope_padded(_rms(_dot(xn, wkpe_ref[...]), krg_ref[...], B_ROPE),
                           cos_ref, sin_lo_ref, sin_hi_ref)
    kpe_ref[...] = kpe_pad[:, 0:B_ROPE]
    zb_ref[:, half:] = _silu(_dot(xn, wbz_ref[:, half:]))
    _expand_kv(ckv_n, kpe_pad, wkvb_ref, kng_ref, k_ref, v_ref)


def _branch_b(xn, p, l, tables, tm, table_tiles, latents):
    m = xn.shape[0]
    depth = p['w_in'].shape[0]
    tab = pl.BlockSpec((tm, LANES), lambda i: (i % table_tiles, 0))
    gains = [p[n] for n in ('b_q_gain', 'b_kv_gain', 'b_kr_gain', 'b_qn_gain', 'b_qr_gain', 'b_kn_gain')]
    acc = [] if latents is None else list(latents)
    n_in = 15
    return pl.pallas_call(
        _branch_b_kernel,
        grid=(m // tm,),
        in_specs=[_rows(tm, D_MODEL),
                  _w_in_block(l, B_Q_RANK + B_KV_RANK, R_CQKV), _w_in_block(l, LANES, R_KPE),
                  _w_in_block(l, B_WIDTH, R_BZ),
                  _layer_block(l, (B_Q_RANK, QK_WIDTH)), _layer_block(l, (B_KV_RANK, 2 * B_WIDTH))]
                 + [_layer_block(l, g.shape[1:]) for g in gains] + [tab, tab, tab]
                 + [pl.BlockSpec(memory_space=pl.ANY)] * len(acc),
        out_specs=[_rows(tm, QK_WIDTH), _rows(tm, QK_WIDTH), _rows(tm, B_WIDTH), _rows(tm, B_WIDTH),
                   pl.BlockSpec((None, tm, B_KV_RANK), lambda i: (l, i, 0)),
                   pl.BlockSpec((None, tm, B_ROPE), lambda i: (l, i, 0))],
        out_shape=[jax.ShapeDtypeStruct((m, QK_WIDTH), BF16),
                   jax.ShapeDtypeStruct((m, QK_WIDTH), BF16),
                   jax.ShapeDtypeStruct((m, B_WIDTH), BF16),
                   jax.ShapeDtypeStruct((m, B_WIDTH), F32),
                   jax.ShapeDtypeStruct((depth, m, B_KV_RANK), F32),
                   jax.ShapeDtypeStruct((depth, m, B_ROPE), F32)],
        input_output_aliases={n_in + j: 4 + j for j in range(len(acc))},
        compiler_params=_params(("parallel",)),
        name="branch_b",
    )(xn, p['w_in'], p['w_in'], p['w_in'], p['wqb'], p['wkvb'], *gains, *tables, *acc)


def _qk(q, k):
    return lax.dot_general(q, k, (((1,), (1,)), ((), ())), preferred_element_type=F32)


EXP2_SCALE = ATTN_SCALE * math.log2(math.e)


def _attn_prompt_kernel(q_ref, k_ref, v_ref, zb_ref, o_ref, vext_ref, *, tq, heads):
    s_len = q_ref.shape[0]
    rowc = lax.broadcasted_iota(jnp.int32, (tq, tq), 0) // CHUNK
    colc = lax.broadcasted_iota(jnp.int32, (tq, tq), 1) // CHUNK
    visible = colc <= rowc
    for hh in range(heads):
        vext_ref[hh, :, 0:B_VDIM] = v_ref[:, hh * B_VDIM:(hh + 1) * B_VDIM]
        vext_ref[hh, :, B_VDIM:] = jnp.ones((s_len, LANES), vext_ref.dtype)
    hs = range(heads)
    qs = [slice(hh * HEAD_PAD, (hh + 1) * HEAD_PAD) for hh in hs]
    vs = [slice(hh * B_VDIM, (hh + 1) * B_VDIM) for hh in hs]

    def scores(qi):
        r0 = qi * tq
        q_t = [q_ref[r0:r0 + tq, qs[hh]] for hh in hs]
        s_diag = [jnp.where(visible, _qk(q_t[hh], k_ref[r0:r0 + tq, qs[hh]]), -jnp.inf) for hh in hs]
        mx = [jnp.max(s_diag[hh], axis=-1, keepdims=True) for hh in hs]
        s_prev = None
        if qi:
            s_prev = [_qk(q_t[hh], k_ref[0:r0, qs[hh]]) for hh in hs]
            mx = [jnp.maximum(mx[hh], jnp.max(s_prev[hh], axis=-1, keepdims=True)) for hh in hs]
        return s_diag, s_prev, mx

    def weighted_values(qi, s_diag, s_prev, mx):
        r0 = qi * tq
        mc = [mx[hh] * EXP2_SCALE for hh in hs]
        p_diag = [jnp.exp2(s_diag[hh] * EXP2_SCALE - mc[hh]).astype(BF16) for hh in hs]
        acc = [_dot(p_diag[hh], vext_ref[hh, r0:r0 + tq, :]) for hh in hs]
        if qi:
            p_prev = [jnp.exp2(s_prev[hh] * EXP2_SCALE - mc[hh]).astype(BF16) for hh in hs]
            acc = [acc[hh] + _dot(p_prev[hh], vext_ref[hh, 0:r0, :]) for hh in hs]
        for hh in hs:
            o = acc[hh][:, 0:B_VDIM] / acc[hh][:, B_VDIM:]
            o_ref[r0:r0 + tq, vs[hh]] = (o * zb_ref[r0:r0 + tq, vs[hh]]).astype(o_ref.dtype)

    for qi in range(s_len // tq):
        weighted_values(qi, *scores(qi))


def _attn_prompt(q, k, v, zb, tq, heads):
    bn, s_len, _ = q.shape
    blk = lambda w: pl.BlockSpec((None, s_len, heads * w), lambda b, h: (b, 0, h))
    return pl.pallas_call(
        functools.partial(_attn_prompt_kernel, tq=tq, heads=heads),
        grid=(bn, B_HEADS // heads),
        in_specs=[blk(HEAD_PAD), blk(HEAD_PAD), blk(B_VDIM), blk(B_VDIM)],
        out_specs=blk(B_VDIM),
        out_shape=jax.ShapeDtypeStruct((bn, s_len, B_WIDTH), BF16),
        scratch_shapes=[pltpu.VMEM((heads, s_len, B_VDIM + LANES), BF16)],
        compiler_params=_params(("parallel", "parallel")),
        name="attn_prompt",
    )(q, k, v, zb)


SAMPLE_HEAD_GROUP = 4


def _attn_sample_kernel(ckv_ref, kpe_ref, ckvn_ref, kpen_ref, qbd_ref, qpe_ref, zb_ref, wkvb_ref, kng_ref,
                        o_ref, kn_ref):
    past, t = ckv_ref.shape[0], ckvn_ref.shape[0]
    ckv = ckv_ref[...].astype(BF16)
    ckvn = ckvn_ref[...].astype(BF16)
    gw = SAMPLE_HEAD_GROUP * B_NOPE
    for g in range(B_HEADS // SAMPLE_HEAD_GROUP):
        kx = _dot(ckv, wkvb_ref[:, g * gw:(g + 1) * gw])
        kxn = _dot(ckvn, wkvb_ref[:, g * gw:(g + 1) * gw])
        for hd in range(SAMPLE_HEAD_GROUP):
            src = slice(hd * B_NOPE, (hd + 1) * B_NOPE)
            dst = slice(g * gw + hd * B_NOPE, g * gw + (hd + 1) * B_NOPE)
            kn_ref[0:past, dst] = _rms(kx[:, src], kng_ref[...]).astype(kn_ref.dtype)
            kn_ref[past:past + t, dst] = _rms(kxn[:, src], kng_ref[...]).astype(kn_ref.dtype)
    qbd, qpe = qbd_ref[...], qpe_ref[...]
    sp = (_qk(qbd, kn_ref[0:past, :]) + _qk(qpe, kpe_ref[...].astype(BF16))) * ATTN_SCALE
    sn = (_qk(qbd, kn_ref[past:past + t, :]) + _qk(qpe, kpen_ref[...].astype(BF16))) * ATTN_SCALE
    mx = jnp.maximum(jnp.max(sp, axis=-1, keepdims=True), jnp.max(sn, axis=-1, keepdims=True))
    pp = jnp.exp(sp - mx)
    pn = jnp.exp(sn - mx)
    inv = 1.0 / (jnp.sum(pp, axis=-1, keepdims=True) + jnp.sum(pn, axis=-1, keepdims=True))
    u = _dot((pp * inv).astype(BF16), ckv) + _dot((pn * inv).astype(BF16), ckvn)
    o_all = _dot(u.astype(BF16), wkvb_ref[:, B_HEADS * B_NOPE:])
    for hd in range(B_HEADS):
        vs = slice(hd * B_VDIM, (hd + 1) * B_VDIM)
        o_ref[:, vs] = (o_all[hd * t:(hd + 1) * t, vs] * zb_ref[:, vs]).astype(o_ref.dtype)


def _attn_sample(q, zb, cache_ckv, cache_kpe, latents, p, l, bn, t):
    past = cache_ckv.shape[1] // bn
    q4 = q.reshape(bn, t, B_HEADS, HEAD_PAD)
    same_head = jnp.eye(B_HEADS, dtype=bool)[None, :, None, :, None]
    qbd = jnp.where(same_head, jnp.swapaxes(q4[..., :B_NOPE], 1, 2)[:, :, :, None, :], 0).reshape(
        bn, B_HEADS * t, B_HEADS * B_NOPE)
    qpe = jnp.transpose(q4[..., B_NOPE:B_QK], (0, 2, 1, 3)).reshape(bn, B_HEADS * t, B_ROPE)
    lrows = lambda n, w: pl.BlockSpec((None, n, w), lambda b: (l, b, 0))
    per_stream = lambda r, c: pl.BlockSpec((None, r, c), lambda b: (b, 0, 0))
    return pl.pallas_call(
        _attn_sample_kernel,
        grid=(bn,),
        in_specs=[lrows(past, B_KV_RANK), lrows(past, B_ROPE), lrows(t, B_KV_RANK), lrows(t, B_ROPE),
                  per_stream(B_HEADS * t, B_HEADS * B_NOPE), per_stream(B_HEADS * t, B_ROPE),
                  _rows(t, B_WIDTH), _layer_block(l, (B_KV_RANK, 2 * B_WIDTH)),
                  _layer_block(l, (1, B_NOPE))],
        out_specs=_rows(t, B_WIDTH),
        out_shape=jax.ShapeDtypeStruct((bn * t, B_WIDTH), BF16),
        scratch_shapes=[pltpu.VMEM((past + t, B_HEADS * B_NOPE), BF16)],
        compiler_params=_params(("parallel",)),
        name="attn_sample",
    )(cache_ckv, cache_kpe, latents[0], latents[1], qbd, qpe, zb, p['wkvb'], p['b_kn_gain'])


def _merge_kernel(xn_ref, ya_ref, yb_ref, yc_ref, wg0_ref, wg1_ref, wg2_ref,
                  wpa_ref, wpb_ref, wpc_ref, h_ref):
    xn = xn_ref[...]
    h = (_sigmoid(_dot(xn, wg0_ref[...])) * _dot(ya_ref[...], wpa_ref[...])
         + _sigmoid(_dot(xn, wg1_ref[...])) * _dot(yb_ref[...], wpb_ref[...])
         + _sigmoid(_dot(xn, wg2_ref[...])) * _dot(yc_ref[...], wpc_ref[...]))
    h_ref[...] = h.astype(h_ref.dtype)


def _merge(xn, ya, yb, yc, p, l, tm, tn):
    m = xn.shape[0]
    nj = D_MODEL // tn
    row = lambda w: pl.BlockSpec((tm, w), lambda i, j: (i, 0))
    colw = lambda k, off: pl.BlockSpec((None, k, tn), lambda i, j: (l, 0, j + off))
    g0 = R_G // tn
    return pl.pallas_call(
        _merge_kernel,
        grid=(m // tm, nj),
        in_specs=[row(D_MODEL), row(A_WIDTH), row(B_WIDTH), row(C_WIDTH),
                  colw(D_MODEL, g0), colw(D_MODEL, g0 + nj), colw(D_MODEL, g0 + 2 * nj),
                  colw(A_WIDTH, 0), colw(B_WIDTH, 0), colw(C_WIDTH, 0)],
        out_specs=pl.BlockSpec((tm, tn), lambda i, j: (i, j)),
        out_shape=jax.ShapeDtypeStruct((m, D_MODEL), BF16),
        compiler_params=_params(("parallel", "parallel")),
        name="merge",
    )(xn, ya, yb, yc, p['w_in'], p['w_in'], p['w_in'], p['w_pa'], p['w_pb'], p['w_pc'])


def _out_kernel(x_ref, h_ref, w_ref, o_ref):
    o_ref[...] = x_ref[...] + _dot(h_ref[...], w_ref[...])


def _out_proj(x, h, p, l, tm):
    m = x.shape[0]
    return pl.pallas_call(
        _out_kernel,
        grid=(m // tm,),
        in_specs=[_rows(tm, D_MODEL), _rows(tm, D_MODEL), _layer_block(l, (D_MODEL, D_MODEL))],
        out_specs=_rows(tm, D_MODEL),
        out_shape=jax.ShapeDtypeStruct((m, D_MODEL), F32),
        compiler_params=_params(("parallel",)),
        name="out_proj",
    )(x, h, p['w_out'])


S_A = 0
S_CQKV = S_A + 3 * A_WIDTH
S_KPE = S_CQKV + B_Q_RANK + B_KV_RANK
S_BZ = S_KPE + B_ROPE
S_C = S_BZ + B_WIDTH
S_G = S_C + 4 * C_WIDTH
S_TOTAL = S_G + N_BRANCH * D_MODEL


RL_ROWS = 64
RL_PIECES = 8
RL_COLS = RL_ROWS * RL_PIECES
RL_SEGMENTS = ((R_C, S_C - R_C), (R_BZ, S_BZ - R_BZ), (R_A, S_A - R_A), (R_CQKV, S_CQKV - R_CQKV),
               (R_KPE, S_KPE - R_KPE), (R_G, S_G - R_G))
RL_KPE_BLOCK = R_KPE // RL_COLS


def _relayout_source_piece(j, k):
    delta = RL_SEGMENTS[0][1] // RL_ROWS
    for start, d in RL_SEGMENTS[1:]:
        delta = jnp.where(j >= start // RL_COLS, d // RL_ROWS, delta)
    return RL_PIECES * j + delta + k


def _relayout_kernel(*refs):
    o_ref = refs[-1]
    x = jnp.concatenate([r[...] for r in refs[:-1]], axis=0)
    row = lax.broadcasted_iota(jnp.int32, (RL_COLS, 1), 0)
    keep = jnp.logical_or(pl.program_id(1) != RL_KPE_BLOCK, row < B_ROPE)
    o_ref[...] = jnp.where(keep, x, 0.0).T.astype(o_ref.dtype)


def _relayout_w_in(w_in):
    depth = w_in.shape[0]
    assert w_in.shape[2] == S_TOTAL and R_TOTAL % RL_COLS == 0
    w_t = jnp.swapaxes(w_in, 1, 2)
    piece = lambda k: pl.BlockSpec((None, RL_ROWS, D_MODEL),
                                   lambda l, j: (l, _relayout_source_piece(j, k), 0))
    return pl.pallas_call(
        _relayout_kernel,
        grid=(depth, R_TOTAL // RL_COLS),
        in_specs=[piece(k) for k in range(RL_PIECES)],
        out_specs=pl.BlockSpec((None, D_MODEL, RL_COLS), lambda l, j: (l, 0, j)),
        out_shape=jax.ShapeDtypeStruct((depth, D_MODEL, R_TOTAL), BF16),
        compiler_params=_params(("parallel", "parallel")),
        name="relayout_w_in",
    )(*([w_t] * RL_PIECES))


def _rope_tables(pos):
    half = B_ROPE // 2
    inv = ROPE_BASE ** (-jnp.arange(half, dtype=F32) / half)
    ang = pos[:, None] * inv[None, :]
    c, s = jnp.cos(ang), jnp.sin(ang)
    z = jnp.zeros_like(c)
    return (jnp.concatenate([c, c, z, z], axis=1),
            jnp.concatenate([-s, z, z, z], axis=1),
            jnp.concatenate([z, s, z, z], axis=1))


def _prepare_params(norm_gain, w_in, a_v_gain, a_ws, a_bias, b_q_gain, b_w_qb, b_kv_gain, b_kr_gain,
                    b_w_kvb, b_qn_gain, b_qr_gain, b_kn_gain, c_conv_w, w_pa, w_pb, w_pc, w_out):
    depth = w_in.shape[0]
    w_r = _relayout_w_in(w_in)
    wqb = jnp.pad(b_w_qb.reshape(depth, B_Q_RANK, B_HEADS, B_QK),
                  ((0, 0), (0, 0), (0, 0), (0, HEAD_PAD - B_QK))).reshape(depth, B_Q_RANK, QK_WIDTH)
    wkv = b_w_kvb.reshape(depth, B_KV_RANK, B_HEADS, B_NOPE + B_VDIM)
    wkvb = jnp.concatenate([wkv[..., :B_NOPE].reshape(depth, B_KV_RANK, B_HEADS * B_NOPE),
                            wkv[..., B_NOPE:].reshape(depth, B_KV_RANK, B_WIDTH)], axis=2)
    row = lambda g: g[:, None, :]
    pad_rope = lambda g: jnp.pad(g, ((0, 0), (0, LANES - B_ROPE)))[:, None, :]
    return dict(
        norm=row(norm_gain), w_in=w_r, a_v_gain=row(a_v_gain), a_ws=a_ws, a_bias=a_bias,
        wqb=wqb.astype(BF16), wkvb=wkvb.astype(BF16),
        b_q_gain=row(b_q_gain), b_kv_gain=row(b_kv_gain), b_kr_gain=pad_rope(b_kr_gain),
        b_qn_gain=row(b_qn_gain), b_qr_gain=pad_rope(b_qr_gain), b_kn_gain=row(b_kn_gain),
        conv_w=c_conv_w,
        w_pa=w_pa.astype(BF16), w_pb=w_pb.astype(BF16), w_pc=w_pc.astype(BF16),
        w_out=w_out.astype(BF16))


def _finish_layer(x, xn, ya, yb, yc, p, l, tm_merge, tm_out):
    h = _merge(xn, ya, yb, yc, p, l, tm_merge, 256)
    return _out_proj(x, h, p, l, tm_out)


def _prompt_layer(x, p, l, tables, bn, s_len, latents, zero_state):
    tm = 1024
    xn, ya = _branch_a(x, p, l, p['a_ws'], jnp.swapaxes(p['a_bias'], 1, 2), tm, False)
    tm_c = 1024
    yc, conv_new = _branch_c(xn, p, l, zero_state, 0, tm_c, tm_c, s_len // tm_c, 512)
    tm_b = 256
    q, k, v, zb, ckv, kpe = _branch_b(xn, p, l, tables, tm_b, s_len // tm_b, latents)
    r3 = lambda a: a.reshape(bn, s_len, a.shape[-1])
    yb = _attn_prompt(r3(q), r3(k), r3(v), r3(zb), 256, 4).reshape(bn * s_len, B_WIDTH)
    out = _finish_layer(x, xn, ya, yb, yc, p, l, 1024, tm)
    return out, (ckv, kpe), conv_new


def _sample_layer(x, p, l, tables, bn, t, ws_blk, bias_t, state_conv, cache_ckv, cache_kpe, latents):
    m = bn * t
    xn, ya, a_vn = _branch_a(x, p, l, ws_blk, bias_t, m, True)
    yc, conv_new = _branch_c(xn, p, l, state_conv, l, m, t, 0, C_WIDTH)
    q, k, v, zb, ckv, kpe = _branch_b(xn, p, l, tables, m, 1, latents)
    yb = _attn_sample(q, zb, cache_ckv, cache_kpe, (ckv, kpe), p, l, bn, t)
    out = _finish_layer(x, xn, ya, yb, yc, p, l, m, m)
    return out, (ckv, kpe), conv_new, a_vn


def kernel(x_prompt, x_sample, cache_ckv, cache_kpe, state_conv, norm_gain, w_in, a_v_gain, a_ws, a_bias,
           b_q_gain, b_w_qb, b_kv_gain, b_kr_gain, b_w_kvb, b_qn_gain, b_qr_gain, b_kn_gain, c_conv_w,
           w_pa, w_pb, w_pc, w_out):
    depth = w_in.shape[0]
    bn, s_len, _ = x_prompt.shape
    dbn, t, _ = x_sample.shape
    past = cache_ckv.shape[2]
    assert A_CHUNK % t == 0 and dbn * t == A_CHUNK and past % A_CHUNK == 0

    p = _prepare_params(norm_gain, w_in, a_v_gain, a_ws, a_bias, b_q_gain, b_w_qb, b_kv_gain, b_kr_gain,
                        b_w_kvb, b_qn_gain, b_qr_gain, b_kn_gain, c_conv_w, w_pa, w_pb, w_pc, w_out)
    tab_p = _rope_tables(jnp.arange(s_len, dtype=F32))
    tab_s = tuple(jnp.tile(tb, (dbn, 1)) for tb in _rope_tables(past + jnp.arange(t, dtype=F32)))

    hp = x_prompt.reshape(bn * s_len, D_MODEL)
    zero_state = jnp.zeros((1, bn, C_CONV - 1, C_WIDTH), F32)
    lat_p, conv_p = None, []
    for l in range(depth):
        hp, lat_p, cst = _prompt_layer(hp, p, l, tab_p, bn, s_len, lat_p, zero_state)
        conv_p.append(cst)

    m_s = dbn * t
    ws_blk = jnp.einsum('ab,lgts->lgatbs', jnp.eye(dbn, dtype=F32), a_ws[:, :, :t, :t]).reshape(
        depth, A_GROUPS, m_s, m_s)
    bias_s = jnp.swapaxes(jnp.tile(a_bias[:, :, :t], (1, 1, dbn)), 1, 2)
    ckv_c = cache_ckv.reshape(depth, dbn * past, B_KV_RANK)
    kpe_c = cache_kpe.reshape(depth, dbn * past, B_ROPE)
    hs = x_sample.reshape(m_s, D_MODEL)
    lat_s, conv_s, av_s = None, [], []
    for l in range(depth):
        hs, lat_s, cst, avn = _sample_layer(hs, p, l, tab_s, dbn, t, ws_blk, bias_s, state_conv,
                                            ckv_c, kpe_c, lat_s)
        conv_s.append(cst)
        av_s.append(avn.reshape(dbn, t, A_WIDTH))

    return (hp.reshape(bn, s_len, D_MODEL), hs.reshape(dbn, t, D_MODEL),
            lat_p[0].reshape(depth, bn, s_len, B_KV_RANK), lat_p[1].reshape(depth, bn, s_len, B_ROPE),
            jnp.stack(conv_p),
            lat_s[0].reshape(depth, dbn, t, B_KV_RANK), lat_s[1].reshape(depth, dbn, t, B_ROPE),
            jnp.stack(conv_s), jnp.stack(av_s))
```

```python
import functools
import math

import jax
import jax.numpy as jnp
from jax import lax
from jax.experimental import pallas as pl
from jax.experimental.pallas import tpu as pltpu

D_MODEL = 2048
CHUNK = 64
EPS = 1e-6
A_WIDTH = 1024
A_GROUPS = 4
A_GROUP_DIM = A_WIDTH // A_GROUPS
A_CHUNK = 128
B_HEADS = 16
B_Q_RANK = 512
B_KV_RANK = 512
B_NOPE = 128
B_ROPE = 64
B_VDIM = 128
B_QK = B_NOPE + B_ROPE
B_WIDTH = B_HEADS * B_VDIM
ATTN_SCALE = B_QK ** -0.5
ROPE_BASE = 10000.0
C_WIDTH = 1024
C_CONV = 3
N_BRANCH = 3

LANES = 128
HEAD_PAD = 2 * LANES
QK_WIDTH = B_HEADS * HEAD_PAD

R_C = 0
R_BZ = R_C + 4 * C_WIDTH
R_A = R_BZ + B_WIDTH
R_CQKV = R_A + 3 * A_WIDTH
R_KPE = R_CQKV + B_Q_RANK + B_KV_RANK
R_G = R_KPE + 4 * LANES
R_TOTAL = R_G + N_BRANCH * D_MODEL

VMEM_LIMIT = 56 * 1024 * 1024
BF16 = jnp.bfloat16
F32 = jnp.float32


def _dot(a, b):
    return jnp.dot(a, b, preferred_element_type=F32)


def _rms(x, gain, width=None):
    width = x.shape[-1] if width is None else width
    ms = jnp.sum(x * x, axis=-1, keepdims=True) * (1.0 / width)
    return x * lax.rsqrt(ms + EPS) * gain


def _silu(x):
    return x * (1.0 / (1.0 + jnp.exp(-x)))


def _sigmoid(x):
    return 1.0 / (1.0 + jnp.exp(-x))


def _params(sem):
    return pltpu.CompilerParams(dimension_semantics=sem, vmem_limit_bytes=VMEM_LIMIT)


def _layer_block(l, shape, blk=0):
    nd = len(shape)
    return pl.BlockSpec((None,) + tuple(shape), lambda *_: (l,) + (0,) * (nd - 1) + (blk,),
                        pipeline_mode=pl.Buffered(1))


def _w_in_block(l, width, offset):
    return _layer_block(l, (D_MODEL, width), offset // width)


def _rows(tm, width):
    return pl.BlockSpec((tm, width), lambda i: (i, 0))


def _branch_a_kernel(x_ref, ng_ref, w_ref, gain_ref, ws_ref, bias_ref, xn_ref, ya_ref, *vn_ref, tm):
    xn = _rms(x_ref[...], ng_ref[...]).astype(xn_ref.dtype)
    xn_ref[...] = xn
    row = lax.broadcasted_iota(jnp.int32, (A_CHUNK, A_CHUNK), 0)
    col = lax.broadcasted_iota(jnp.int32, (A_CHUNK, A_CHUNK), 1)
    for g in range(A_GROUPS):
        lo, hi = g * A_GROUP_DIM, (g + 1) * A_GROUP_DIM
        u = _dot(xn, w_ref[:, lo:hi])
        v = _dot(xn, w_ref[:, A_WIDTH + lo:A_WIDTH + hi])
        z = _dot(xn, w_ref[:, 2 * A_WIDTH + lo:2 * A_WIDTH + hi])
        vn = _rms(v, gain_ref[:, lo:hi])
        if vn_ref:
            vn_ref[0][:, lo:hi] = vn
        gate = u * _silu(z)
        ws = jnp.where(col <= row, ws_ref[g], 0.0).astype(BF16)
        bias = bias_ref[:, g:g + 1]
        vnb = vn.astype(BF16)
        for c in range(tm // A_CHUNK):
            r0, r1 = c * A_CHUNK, (c + 1) * A_CHUNK
            mixed = _dot(ws, vnb[r0:r1]) + bias
            ya_ref[r0:r1, lo:hi] = (gate[r0:r1] * mixed).astype(ya_ref.dtype)


def _branch_a(x, p, l, ws, bias_t, tm, emit_vn):
    m = x.shape[0]
    out_shape = [jax.ShapeDtypeStruct((m, D_MODEL), BF16), jax.ShapeDtypeStruct((m, A_WIDTH), BF16)]
    out_specs = [_rows(tm, D_MODEL), _rows(tm, A_WIDTH)]
    if emit_vn:
        out_shape.append(jax.ShapeDtypeStruct((m, A_WIDTH), F32))
        out_specs.append(_rows(tm, A_WIDTH))
    return pl.pallas_call(
        functools.partial(_branch_a_kernel, tm=tm),
        grid=(m // tm,),
        in_specs=[_rows(tm, D_MODEL), _layer_block(l, (1, D_MODEL)), _w_in_block(l, 3 * A_WIDTH, R_A),
                  _layer_block(l, (1, A_WIDTH)), _layer_block(l, ws.shape[1:]),
                  _layer_block(l, bias_t.shape[1:])],
        out_specs=out_specs,
        out_shape=out_shape,
        compiler_params=_params(("parallel",)),
        name="branch_a",
    )(x, p['norm'], p['w_in'], p['a_v_gain'], ws, bias_t)


def _branch_c_kernel(xn_ref, w_ref, cw_ref, state_ref, yc_ref, new_ref, pad_ref, *, seg, carry, cb):
    tm = xn_ref.shape[0]
    xn = xn_ref[...]
    stride = seg + 8
    if carry:
        first = pl.program_id(0) % carry == 0

        @pl.when(first)
        def _():
            pad_ref[6:8, :] = state_ref[0]
    for j in range(C_WIDTH // cb):
        cs = slice(j * cb, (j + 1) * cb)
        h, b, c, z = [_dot(xn, w_ref[:, k * C_WIDTH + j * cb:k * C_WIDTH + (j + 1) * cb])
                      for k in range(4)]
        hc = c * h
        gate = b * _silu(z)
        w0, w1, w2 = cw_ref[0:1, cs], cw_ref[1:2, cs], cw_ref[2:3, cs]
        for s in range(tm // seg):
            base = s * stride
            rows = slice(s * seg, (s + 1) * seg)
            if not carry:
                pad_ref[base + 6:base + 8, cs] = state_ref[s, :, cs]
            pad_ref[base + 8:base + 8 + seg, cs] = hc[rows]
            y = (w2 * hc[rows]
                 + w1 * pad_ref[base + 7:base + 7 + seg, cs]
                 + w0 * pad_ref[base + 6:base + 6 + seg, cs])
            yc_ref[rows, cs] = (gate[rows] * y).astype(yc_ref.dtype)
            new_ref[s, :, cs] = hc[(s + 1) * seg - 2:(s + 1) * seg]
        if carry:
            pad_ref[6:8, cs] = hc[tm - 2:tm]


def _branch_c(xn, p, l, state, state_layer, tm, seg, tiles_per_stream, cb):
    m = xn.shape[0]
    n_streams = state.shape[1]
    nseg = tm // seg
    blk = (nseg, C_CONV - 1, C_WIDTH)
    if tiles_per_stream:
        st_in = pl.BlockSpec((None,) + blk, lambda i: (state_layer, i // tiles_per_stream, 0, 0))
        st_out = pl.BlockSpec(blk, lambda i: (i // tiles_per_stream, 0, 0))
    else:
        st_in = pl.BlockSpec((None,) + blk, lambda i: (state_layer, i, 0, 0))
        st_out = pl.BlockSpec(blk, lambda i: (i, 0, 0))
    return pl.pallas_call(
        functools.partial(_branch_c_kernel, seg=seg, carry=tiles_per_stream, cb=cb),
        grid=(m // tm,),
        in_specs=[_rows(tm, D_MODEL), _w_in_block(l, 4 * C_WIDTH, R_C),
                  _layer_block(l, (C_CONV, C_WIDTH)), st_in],
        out_specs=[_rows(tm, C_WIDTH), st_out],
        out_shape=[jax.ShapeDtypeStruct((m, C_WIDTH), BF16),
                   jax.ShapeDtypeStruct((n_streams, C_CONV - 1, C_WIDTH), F32)],
        scratch_shapes=[pltpu.VMEM((nseg * (seg + 8), C_WIDTH), F32)],
        compiler_params=_params(("arbitrary",)),
        name="branch_c",
    )(xn, p['w_in'], p['conv_w'], state)


def _rope_padded(x, cos_ref, sin_lo_ref, sin_hi_ref):
    return (x * cos_ref[...]
            + pltpu.roll(x, LANES - B_ROPE // 2, 1) * sin_lo_ref[...]
            + pltpu.roll(x, B_ROPE // 2, 1) * sin_hi_ref[...])


def _expand_kv(ckv_n, kpe_pad, wkvb_ref, kng_ref, k_ref, v_ref):
    kv = _dot(ckv_n.astype(BF16), wkvb_ref[...])
    v_ref[...] = kv[:, B_HEADS * B_NOPE:].astype(v_ref.dtype)
    kpe_b = kpe_pad.astype(k_ref.dtype)
    kn = [_rms(kv[:, hd * B_NOPE:(hd + 1) * B_NOPE], kng_ref[...]) for hd in range(B_HEADS)]
    for hd in range(B_HEADS):
        k_ref[:, hd * HEAD_PAD:hd * HEAD_PAD + LANES] = kn[hd].astype(k_ref.dtype)
        k_ref[:, hd * HEAD_PAD + LANES:(hd + 1) * HEAD_PAD] = kpe_b


def _branch_b_kernel(xn_ref, wcqkv_ref, wkpe_ref, wbz_ref, wqb_ref, wkvb_ref, qg_ref, kvg_ref, krg_ref,
                     qng_ref, qrg_ref, kng_ref, cos_ref, sin_lo_ref, sin_hi_ref, *rest):
    q_ref, k_ref, v_ref, zb_ref, ckv_ref, kpe_ref = rest[-6:]
    xn = xn_ref[...]
    half = B_WIDTH // 2
    cq = _rms(_dot(xn, wcqkv_ref[:, 0:B_Q_RANK]), qg_ref[...])
    zb_ref[:, 0:half] = _silu(_dot(xn, wbz_ref[:, 0:half]))
    q = _dot(cq.astype(BF16), wqb_ref[...])
    heads = range(B_HEADS)
    qa = [_rms(q[:, hd * HEAD_PAD:hd * HEAD_PAD + LANES], qng_ref[...]) for hd in heads]
    qb = [_rms(q[:, hd * HEAD_PAD + LANES:(hd + 1) * HEAD_PAD], qrg_ref[...], B_ROPE) for hd in heads]
    qb = [_rope_padded(qb[hd], cos_ref, sin_lo_ref, sin_hi_ref) for hd in heads]
    for hd in heads:
        q_ref[:, hd * HEAD_PAD:hd * HEAD_PAD + LANES] = qa[hd].astype(q_ref.dtype)
        q_ref[:, hd * HEAD_PAD + LANES:(hd + 1) * HEAD_PAD] = qb[hd].astype(q_ref.dtype)
    ckv_n = _rms(_dot(xn, wcqkv_ref[:, B_Q_RANK:B_Q_RANK + B_KV_RANK]), kvg_ref[...])
    ckv_ref[...] = ckv_n
    kpe_pad = _rope_padded(_rms(_dot(xn, wkpe_ref[...]), krg_ref[...], B_ROPE),
                           cos_ref, sin_lo_ref, sin_hi_ref)
    kpe_ref[...] = kpe_pad[:, 0:B_ROPE]
    zb_ref[:, half:] = _silu(_dot(xn, wbz_ref[:, half:]))
    _expand_kv(ckv_n, kpe_pad, wkvb_ref, kng_ref, k_ref, v_ref)


def _branch_b(xn, p, l, tables, tm, table_tiles, latents):
    m = xn.shape[0]
    depth = p['w_in'].shape[0]
    tab = pl.BlockSpec((tm, LANES), lambda i: (i % table_tiles, 0))
    gains = [p[n] for n in ('b_q_gain', 'b_kv_gain', 'b_kr_gain', 'b_qn_gain', 'b_qr_gain', 'b_kn_gain')]
    acc = [] if latents is None else list(latents)
    n_in = 15
    return pl.pallas_call(
        _branch_b_kernel,
        grid=(m // tm,),
        in_specs=[_rows(tm, D_MODEL),
                  _w_in_block(l, B_Q_RANK + B_KV_RANK, R_CQKV), _w_in_block(l, LANES, R_KPE),
                  _w_in_block(l, B_WIDTH, R_BZ),
                  _layer_block(l, (B_Q_RANK, QK_WIDTH)), _layer_block(l, (B_KV_RANK, 2 * B_WIDTH))]
                 + [_layer_block(l, g.shape[1:]) for g in gains] + [tab, tab, tab]
                 + [pl.BlockSpec(memory_space=pl.ANY)] * len(acc),
        out_specs=[_rows(tm, QK_WIDTH), _rows(tm, QK_WIDTH), _rows(tm, B_WIDTH), _rows(tm, B_WIDTH),
                   pl.BlockSpec((None, tm, B_KV_RANK), lambda i: (l, i, 0)),
                   pl.BlockSpec((None, tm, B_ROPE), lambda i: (l, i, 0))],
        out_shape=[jax.ShapeDtypeStruct((m, QK_WIDTH), BF16),
                   jax.ShapeDtypeStruct((m, QK_WIDTH), BF16),
                   jax.ShapeDtypeStruct((m, B_WIDTH), BF16),
                   jax.ShapeDtypeStruct((m, B_WIDTH), F32),
                   jax.ShapeDtypeStruct((depth, m, B_KV_RANK), F32),
                   jax.ShapeDtypeStruct((depth, m, B_ROPE), F32)],
        input_output_aliases={n_in + j: 4 + j for j in range(len(acc))},
        compiler_params=_params(("parallel",)),
        name="branch_b",
    )(xn, p['w_in'], p['w_in'], p['w_in'], p['wqb'], p['wkvb'], *gains, *tables, *acc)


def _qk(q, k):
    return lax.dot_general(q, k, (((1,), (1,)), ((), ())), preferred_element_type=F32)


EXP2_SCALE = ATTN_SCALE * math.log2(math.e)


def _attn_prompt_kernel(q_ref, k_ref, v_ref, zb_ref, o_ref, vext_ref, *, tq, heads):
    s_len = q_ref.shape[0]
    rowc = lax.broadcasted_iota(jnp.int32, (tq, tq), 0) // CHUNK
    colc = lax.broadcasted_iota(jnp.int32, (tq, tq), 1) // CHUNK
    visible = colc <= rowc
    for hh in range(heads):
        vext_ref[hh, :, 0:B_VDIM] = v_ref[:, hh * B_VDIM:(hh + 1) * B_VDIM]
        vext_ref[hh, :, B_VDIM:] = jnp.ones((s_len, LANES), vext_ref.dtype)
    hs = range(heads)
    qs = [slice(hh * HEAD_PAD, (hh + 1) * HEAD_PAD) for hh in hs]
    vs = [slice(hh * B_VDIM, (hh + 1) * B_VDIM) for hh in hs]

    def scores(qi):
        r0 = qi * tq
        q_t = [q_ref[r0:r0 + tq, qs[hh]] for hh in hs]
        s_diag = [jnp.where(visible, _qk(q_t[hh], k_ref[r0:r0 + tq, qs[hh]]), -jnp.inf) for hh in hs]
        mx = [jnp.max(s_diag[hh], axis=-1, keepdims=True) for hh in hs]
        s_prev = None
        if qi:
            s_prev = [_qk(q_t[hh], k_ref[0:r0, qs[hh]]) for hh in hs]
            mx = [jnp.maximum(mx[hh], jnp.max(s_prev[hh], axis=-1, keepdims=True)) for hh in hs]
        return s_diag, s_prev, mx

    def weighted_values(qi, s_diag, s_prev, mx):
        r0 = qi * tq
        mc = [mx[hh] * EXP2_SCALE for hh in hs]
        p_diag = [jnp.exp2(s_diag[hh] * EXP2_SCALE - mc[hh]).astype(BF16) for hh in hs]
        acc = [_dot(p_diag[hh], vext_ref[hh, r0:r0 + tq, :]) for hh in hs]
        if qi:
            p_prev = [jnp.exp2(s_prev[hh] * EXP2_SCALE - mc[hh]).astype(BF16) for hh in hs]
            acc = [acc[hh] + _dot(p_prev[hh], vext_ref[hh, 0:r0, :]) for hh in hs]
        for hh in hs:
            o = acc[hh][:, 0:B_VDIM] / acc[hh][:, B_VDIM:]
            o_ref[r0:r0 + tq, vs[hh]] = (o * zb_ref[r0:r0 + tq, vs[hh]]).astype(o_ref.dtype)

    for qi in range(s_len // tq):
        weighted_values(qi, *scores(qi))


def _attn_prompt(q, k, v, zb, tq, heads):
    bn, s_len, _ = q.shape
    blk = lambda w: pl.BlockSpec((None, s_len, heads * w), lambda b, h: (b, 0, h))
    return pl.pallas_call(
        functools.partial(_attn_prompt_kernel, tq=tq, heads=heads),
        grid=(bn, B_HEADS // heads),
        in_specs=[blk(HEAD_PAD), blk(HEAD_PAD), blk(B_VDIM), blk(B_VDIM)],
        out_specs=blk(B_VDIM),
        out_shape=jax.ShapeDtypeStruct((bn, s_len, B_WIDTH), BF16),
        scratch_shapes=[pltpu.VMEM((heads, s_len, B_VDIM + LANES), BF16)],
        compiler_params=_params(("parallel", "parallel")),
        name="attn_prompt",
    )(q, k, v, zb)


SAMPLE_HEAD_GROUP = 4


def _attn_sample_kernel(ckv_ref, kpe_ref, ckvn_ref, kpen_ref, q_ref, zb_ref, wkvb_ref, kng_ref,
                        o_ref, kn_ref):
    past, t = ckv_ref.shape[0], ckvn_ref.shape[0]
    ckv = ckv_ref[...].astype(BF16)
    ckvn = ckvn_ref[...].astype(BF16)
    gw = SAMPLE_HEAD_GROUP * B_NOPE
    for g in range(B_HEADS // SAMPLE_HEAD_GROUP):
        kx = _dot(ckv, wkvb_ref[:, g * gw:(g + 1) * gw])
        kxn = _dot(ckvn, wkvb_ref[:, g * gw:(g + 1) * gw])
        for hd in range(SAMPLE_HEAD_GROUP):
            src = slice(hd * B_NOPE, (hd + 1) * B_NOPE)
            dst = slice(g * gw + hd * B_NOPE, g * gw + (hd + 1) * B_NOPE)
            kn_ref[0:past, dst] = _rms(kx[:, src], kng_ref[...]).astype(kn_ref.dtype)
            kn_ref[past:past + t, dst] = _rms(kxn[:, src], kng_ref[...]).astype(kn_ref.dtype)
    zero = jnp.zeros((t, B_NOPE), q_ref.dtype)
    qbd = jnp.concatenate(
        [jnp.concatenate([q_ref[:, hd * HEAD_PAD:hd * HEAD_PAD + B_NOPE] if g == hd else zero
                          for g in range(B_HEADS)], axis=1) for hd in range(B_HEADS)], axis=0)
    qpe = jnp.concatenate([q_ref[:, hd * HEAD_PAD + B_NOPE:hd * HEAD_PAD + B_QK]
                           for hd in range(B_HEADS)], axis=0)
    sp = (_qk(qbd, kn_ref[0:past, :]) + _qk(qpe, kpe_ref[...].astype(BF16))) * ATTN_SCALE
    sn = (_qk(qbd, kn_ref[past:past + t, :]) + _qk(qpe, kpen_ref[...].astype(BF16))) * ATTN_SCALE
    mx = jnp.maximum(jnp.max(sp, axis=-1, keepdims=True), jnp.max(sn, axis=-1, keepdims=True))
    pp = jnp.exp(sp - mx)
    pn = jnp.exp(sn - mx)
    inv = 1.0 / (jnp.sum(pp, axis=-1, keepdims=True) + jnp.sum(pn, axis=-1, keepdims=True))
    u = _dot((pp * inv).astype(BF16), ckv) + _dot((pn * inv).astype(BF16), ckvn)
    o_all = _dot(u.astype(BF16), wkvb_ref[:, B_HEADS * B_NOPE:])
    for hd in range(B_HEADS):
        vs = slice(hd * B_VDIM, (hd + 1) * B_VDIM)
        o_ref[:, vs] = (o_all[hd * t:(hd + 1) * t, vs] * zb_ref[:, vs]).astype(o_ref.dtype)


def _attn_sample(q, zb, cache_ckv, cache_kpe, latents, p, l, bn, t):
    past = cache_ckv.shape[1] // bn
    lrows = lambda n, w: pl.BlockSpec((None, n, w), lambda b: (l, b, 0))
    return pl.pallas_call(
        _attn_sample_kernel,
        grid=(bn,),
        in_specs=[lrows(past, B_KV_RANK), lrows(past, B_ROPE), lrows(t, B_KV_RANK), lrows(t, B_ROPE),
                  _rows(t, QK_WIDTH), _rows(t, B_WIDTH), _layer_block(l, (B_KV_RANK, 2 * B_WIDTH)),
                  _layer_block(l, (1, B_NOPE))],
        out_specs=_rows(t, B_WIDTH),
        out_shape=jax.ShapeDtypeStruct((bn * t, B_WIDTH), BF16),
        scratch_shapes=[pltpu.VMEM((past + t, B_HEADS * B_NOPE), BF16)],
        compiler_params=_params(("parallel",)),
        name="attn_sample",
    )(cache_ckv, cache_kpe, latents[0], latents[1], q, zb, p['wkvb'], p['b_kn_gain'])


def _merge_kernel(xn_ref, ya_ref, yb_ref, yc_ref, wg0_ref, wg1_ref, wg2_ref,
                  wpa_ref, wpb_ref, wpc_ref, h_ref):
    xn = xn_ref[...]
    h = (_sigmoid(_dot(xn, wg0_ref[...])) * _dot(ya_ref[...], wpa_ref[...])
         + _sigmoid(_dot(xn, wg1_ref[...])) * _dot(yb_ref[...], wpb_ref[...])
         + _sigmoid(_dot(xn, wg2_ref[...])) * _dot(yc_ref[...], wpc_ref[...]))
    h_ref[...] = h.astype(h_ref.dtype)


def _merge(xn, ya, yb, yc, p, l, tm, tn):
    m = xn.shape[0]
    nj = D_MODEL // tn
    row = lambda w: pl.BlockSpec((tm, w), lambda i, j: (i, 0))
    colw = lambda k, off: pl.BlockSpec((None, k, tn), lambda i, j: (l, 0, j + off))
    g0 = R_G // tn
    return pl.pallas_call(
        _merge_kernel,
        grid=(m // tm, nj),
        in_specs=[row(D_MODEL), row(A_WIDTH), row(B_WIDTH), row(C_WIDTH),
                  colw(D_MODEL, g0), colw(D_MODEL, g0 + nj), colw(D_MODEL, g0 + 2 * nj),
                  colw(A_WIDTH, 0), colw(B_WIDTH, 0), colw(C_WIDTH, 0)],
        out_specs=pl.BlockSpec((tm, tn), lambda i, j: (i, j)),
        out_shape=jax.ShapeDtypeStruct((m, D_MODEL), BF16),
        compiler_params=_params(("parallel", "parallel")),
        name="merge",
    )(xn, ya, yb, yc, p['w_in'], p['w_in'], p['w_in'], p['w_pa'], p['w_pb'], p['w_pc'])


def _out_kernel(x_ref, h_ref, w_ref, o_ref):
    o_ref[...] = x_ref[...] + _dot(h_ref[...], w_ref[...])


def _out_proj(x, h, p, l, tm):
    m = x.shape[0]
    return pl.pallas_call(
        _out_kernel,
        grid=(m // tm,),
        in_specs=[_rows(tm, D_MODEL), _rows(tm, D_MODEL), _layer_block(l, (D_MODEL, D_MODEL))],
        out_specs=_rows(tm, D_MODEL),
        out_shape=jax.ShapeDtypeStruct((m, D_MODEL), F32),
        compiler_params=_params(("parallel",)),
        name="out_proj",
    )(x, h, p['w_out'])


S_A = 0
S_CQKV = S_A + 3 * A_WIDTH
S_KPE = S_CQKV + B_Q_RANK + B_KV_RANK
S_BZ = S_KPE + B_ROPE
S_C = S_BZ + B_WIDTH
S_G = S_C + 4 * C_WIDTH
S_TOTAL = S_G + N_BRANCH * D_MODEL


RL_ROWS = 64
RL_PIECES = 8
RL_COLS = RL_ROWS * RL_PIECES
RL_SEGMENTS = ((R_C, S_C - R_C), (R_BZ, S_BZ - R_BZ), (R_A, S_A - R_A), (R_CQKV, S_CQKV - R_CQKV),
               (R_KPE, S_KPE - R_KPE), (R_G, S_G - R_G))
RL_KPE_BLOCK = R_KPE // RL_COLS


def _relayout_source_piece(j, k):
    delta = RL_SEGMENTS[0][1] // RL_ROWS
    for start, d in RL_SEGMENTS[1:]:
        delta = jnp.where(j >= start // RL_COLS, d // RL_ROWS, delta)
    return RL_PIECES * j + delta + k


def _relayout_kernel(*refs):
    o_ref = refs[-1]
    x = jnp.concatenate([r[...] for r in refs[:-1]], axis=0)
    row = lax.broadcasted_iota(jnp.int32, (RL_COLS, 1), 0)
    keep = jnp.logical_or(pl.program_id(1) != RL_KPE_BLOCK, row < B_ROPE)
    o_ref[...] = jnp.where(keep, x, 0.0).T.astype(o_ref.dtype)


def _relayout_w_in(w_in):
    depth = w_in.shape[0]
    assert w_in.shape[2] == S_TOTAL and R_TOTAL % RL_COLS == 0
    w_t = jnp.swapaxes(w_in, 1, 2)
    piece = lambda k: pl.BlockSpec((None, RL_ROWS, D_MODEL),
                                   lambda l, j: (l, _relayout_source_piece(j, k), 0))
    return pl.pallas_call(
        _relayout_kernel,
        grid=(depth, R_TOTAL // RL_COLS),
        in_specs=[piece(k) for k in range(RL_PIECES)],
        out_specs=pl.BlockSpec((None, D_MODEL, RL_COLS), lambda l, j: (l, 0, j)),
        out_shape=jax.ShapeDtypeStruct((depth, D_MODEL, R_TOTAL), BF16),
        compiler_params=_params(("parallel", "parallel")),
        name="relayout_w_in",
    )(*([w_t] * RL_PIECES))


def _rope_tables(pos):
    half = B_ROPE // 2
    inv = ROPE_BASE ** (-jnp.arange(half, dtype=F32) / half)
    ang = pos[:, None] * inv[None, :]
    c, s = jnp.cos(ang), jnp.sin(ang)
    z = jnp.zeros_like(c)
    return (jnp.concatenate([c, c, z, z], axis=1),
            jnp.concatenate([-s, z, z, z], axis=1),
            jnp.concatenate([z, s, z, z], axis=1))


def _prepare_params(norm_gain, w_in, a_v_gain, a_ws, a_bias, b_q_gain, b_w_qb, b_kv_gain, b_kr_gain,
                    b_w_kvb, b_qn_gain, b_qr_gain, b_kn_gain, c_conv_w, w_pa, w_pb, w_pc, w_out):
    depth = w_in.shape[0]
    w_r = _relayout_w_in(w_in)
    wqb = jnp.pad(b_w_qb.reshape(depth, B_Q_RANK, B_HEADS, B_QK),
                  ((0, 0), (0, 0), (0, 0), (0, HEAD_PAD - B_QK))).reshape(depth, B_Q_RANK, QK_WIDTH)
    wkv = b_w_kvb.reshape(depth, B_KV_RANK, B_HEADS, B_NOPE + B_VDIM)
    wkvb = jnp.concatenate([wkv[..., :B_NOPE].reshape(depth, B_KV_RANK, B_HEADS * B_NOPE),
                            wkv[..., B_NOPE:].reshape(depth, B_KV_RANK, B_WIDTH)], axis=2)
    row = lambda g: g[:, None, :]
    pad_rope = lambda g: jnp.pad(g, ((0, 0), (0, LANES - B_ROPE)))[:, None, :]
    return dict(
        norm=row(norm_gain), w_in=w_r, a_v_gain=row(a_v_gain), a_ws=a_ws, a_bias=a_bias,
        wqb=wqb.astype(BF16), wkvb=wkvb.astype(BF16),
        b_q_gain=row(b_q_gain), b_kv_gain=row(b_kv_gain), b_kr_gain=pad_rope(b_kr_gain),
        b_qn_gain=row(b_qn_gain), b_qr_gain=pad_rope(b_qr_gain), b_kn_gain=row(b_kn_gain),
        conv_w=c_conv_w,
        w_pa=w_pa.astype(BF16), w_pb=w_pb.astype(BF16), w_pc=w_pc.astype(BF16),
        w_out=w_out.astype(BF16))


def _finish_layer(x, xn, ya, yb, yc, p, l, tm_merge, tm_out):
    h = _merge(xn, ya, yb, yc, p, l, tm_merge, 256)
    return _out_proj(x, h, p, l, tm_out)


def _prompt_layer(x, p, l, tables, bn, s_len, latents, zero_state):
    tm = 1024
    xn, ya = _branch_a(x, p, l, p['a_ws'], jnp.swapaxes(p['a_bias'], 1, 2), tm, False)
    tm_c = 1024
    yc, conv_new = _branch_c(xn, p, l, zero_state, 0, tm_c, tm_c, s_len // tm_c, 512)
    tm_b = 256
    q, k, v, zb, ckv, kpe = _branch_b(xn, p, l, tables, tm_b, s_len // tm_b, latents)
    r3 = lambda a: a.reshape(bn, s_len, a.shape[-1])
    yb = _attn_prompt(r3(q), r3(k), r3(v), r3(zb), 256, 4).reshape(bn * s_len, B_WIDTH)
    out = _finish_layer(x, xn, ya, yb, yc, p, l, 1024, tm)
    return out, (ckv, kpe), conv_new


def _sample_layer(x, p, l, tables, bn, t, ws_blk, bias_t, state_conv, cache_ckv, cache_kpe, latents):
    m = bn * t
    xn, ya, a_vn = _branch_a(x, p, l, ws_blk, bias_t, m, True)
    yc, conv_new = _branch_c(xn, p, l, state_conv, l, m, t, 0, C_WIDTH)
    q, k, v, zb, ckv, kpe = _branch_b(xn, p, l, tables, m, 1, latents)
    yb = _attn_sample(q, zb, cache_ckv, cache_kpe, (ckv, kpe), p, l, bn, t)
    out = _finish_layer(x, xn, ya, yb, yc, p, l, m, m)
    return out, (ckv, kpe), conv_new, a_vn


def kernel(x_prompt, x_sample, cache_ckv, cache_kpe, state_conv, norm_gain, w_in, a_v_gain, a_ws, a_bias,
           b_q_gain, b_w_qb, b_kv_gain, b_kr_gain, b_w_kvb, b_qn_gain, b_qr_gain, b_kn_gain, c_conv_w,
           w_pa, w_pb, w_pc, w_out):
    depth = w_in.shape[0]
    bn, s_len, _ = x_prompt.shape
    dbn, t, _ = x_sample.shape
    past = cache_ckv.shape[2]
    assert A_CHUNK % t == 0 and dbn * t == A_CHUNK and past % A_CHUNK == 0

    p = _prepare_params(norm_gain, w_in, a_v_gain, a_ws, a_bias, b_q_gain, b_w_qb, b_kv_gain, b_kr_gain,
                        b_w_kvb, b_qn_gain, b_qr_gain, b_kn_gain, c_conv_w, w_pa, w_pb, w_pc, w_out)
    tab_p = _rope_tables(jnp.arange(s_len, dtype=F32))
    tab_s = tuple(jnp.tile(tb, (dbn, 1)) for tb in _rope_tables(past + jnp.arange(t, dtype=F32)))

    hp = x_prompt.reshape(bn * s_len, D_MODEL)
    zero_state = jnp.zeros((1, bn, C_CONV - 1, C_WIDTH), F32)
    lat_p, conv_p = None, []
    for l in range(depth):
        hp, lat_p, cst = _prompt_layer(hp, p, l, tab_p, bn, s_len, lat_p, zero_state)
        conv_p.append(cst)

    m_s = dbn * t
    ws_blk = jnp.einsum('ab,lgts->lgatbs', jnp.eye(dbn, dtype=F32), a_ws[:, :, :t, :t]).reshape(
        depth, A_GROUPS, m_s, m_s)
    bias_s = jnp.swapaxes(jnp.tile(a_bias[:, :, :t], (1, 1, dbn)), 1, 2)
    ckv_c = cache_ckv.reshape(depth, dbn * past, B_KV_RANK)
    kpe_c = cache_kpe.reshape(depth, dbn * past, B_ROPE)
    hs = x_sample.reshape(m_s, D_MODEL)
    lat_s, conv_s, av_s = None, [], []
    for l in range(depth):
        hs, lat_s, cst, avn = _sample_layer(hs, p, l, tab_s, dbn, t, ws_blk, bias_s, state_conv,
                                            ckv_c, kpe_c, lat_s)
        conv_s.append(cst)
        av_s.append(avn.reshape(dbn, t, A_WIDTH))

    return (hp.reshape(bn, s_len, D_MODEL), hs.reshape(dbn, t, D_MODEL),
            lat_p[0].reshape(depth, bn, s_len, B_KV_RANK), lat_p[1].reshape(depth, bn, s_len, B_ROPE),
            jnp.stack(conv_p),
            lat_s[0].reshape(depth, dbn, t, B_KV_RANK), lat_s[1].reshape(depth, dbn, t, B_ROPE),
            jnp.stack(conv_s), jnp.stack(av_s))
```

```python
import functools
import math

import jax
import jax.numpy as jnp
from jax import lax
from jax.experimental import pallas as pl
from jax.experimental.pallas import tpu as pltpu

D_MODEL = 2048
CHUNK = 64
EPS = 1e-6
A_WIDTH = 1024
A_GROUPS = 4
A_GROUP_DIM = A_WIDTH // A_GROUPS
A_CHUNK = 128
B_HEADS = 16
B_Q_RANK = 512
B_KV_RANK = 512
B_NOPE = 128
B_ROPE = 64
B_VDIM = 128
B_QK = B_NOPE + B_ROPE
B_WIDTH = B_HEADS * B_VDIM
ATTN_SCALE = B_QK ** -0.5
ROPE_BASE = 10000.0
C_WIDTH = 1024
C_CONV = 3
N_BRANCH = 3

LANES = 128
HEAD_PAD = 2 * LANES
QK_WIDTH = B_HEADS * HEAD_PAD

R_C = 0
R_BZ = R_C + 4 * C_WIDTH
R_A = R_BZ + B_WIDTH
R_CQKV = R_A + 3 * A_WIDTH
R_KPE = R_CQKV + B_Q_RANK + B_KV_RANK
R_G = R_KPE + 4 * LANES
R_TOTAL = R_G + N_BRANCH * D_MODEL

VMEM_LIMIT = 56 * 1024 * 1024
BF16 = jnp.bfloat16
F32 = jnp.float32


def _dot(a, b):
    return jnp.dot(a, b, preferred_element_type=F32)


def _rms(x, gain, width=None):
    width = x.shape[-1] if width is None else width
    ms = jnp.sum(x * x, axis=-1, keepdims=True) * (1.0 / width)
    return x * lax.rsqrt(ms + EPS) * gain


def _silu(x):
    return x * (1.0 / (1.0 + jnp.exp(-x)))


def _sigmoid(x):
    return 1.0 / (1.0 + jnp.exp(-x))


def _params(sem):
    return pltpu.CompilerParams(dimension_semantics=sem, vmem_limit_bytes=VMEM_LIMIT)


def _layer_block(l, shape, blk=0):
    nd = len(shape)
    return pl.BlockSpec((None,) + tuple(shape), lambda *_: (l,) + (0,) * (nd - 1) + (blk,),
                        pipeline_mode=pl.Buffered(1))


def _w_in_block(l, width, offset):
    return _layer_block(l, (D_MODEL, width), offset // width)


def _rows(tm, width):
    return pl.BlockSpec((tm, width), lambda i: (i, 0))


def _branch_a_kernel(x_ref, ng_ref, w_ref, gain_ref, ws_ref, bias_ref, xn_ref, ya_ref, *vn_ref, tm):
    xn = _rms(x_ref[...], ng_ref[...]).astype(xn_ref.dtype)
    xn_ref[...] = xn
    row = lax.broadcasted_iota(jnp.int32, (A_CHUNK, A_CHUNK), 0)
    col = lax.broadcasted_iota(jnp.int32, (A_CHUNK, A_CHUNK), 1)
    for g in range(A_GROUPS):
        lo, hi = g * A_GROUP_DIM, (g + 1) * A_GROUP_DIM
        u = _dot(xn, w_ref[:, lo:hi])
        v = _dot(xn, w_ref[:, A_WIDTH + lo:A_WIDTH + hi])
        z = _dot(xn, w_ref[:, 2 * A_WIDTH + lo:2 * A_WIDTH + hi])
        vn = _rms(v, gain_ref[:, lo:hi])
        if vn_ref:
            vn_ref[0][:, lo:hi] = vn
        gate = u * _silu(z)
        ws = jnp.where(col <= row, ws_ref[g], 0.0).astype(BF16)
        bias = bias_ref[:, g:g + 1]
        vnb = vn.astype(BF16)
        for c in range(tm // A_CHUNK):
            r0, r1 = c * A_CHUNK, (c + 1) * A_CHUNK
            mixed = _dot(ws, vnb[r0:r1]) + bias
            ya_ref[r0:r1, lo:hi] = (gate[r0:r1] * mixed).astype(ya_ref.dtype)


def _branch_a(x, p, l, ws, bias_t, tm, emit_vn):
    m = x.shape[0]
    out_shape = [jax.ShapeDtypeStruct((m, D_MODEL), BF16), jax.ShapeDtypeStruct((m, A_WIDTH), BF16)]
    out_specs = [_rows(tm, D_MODEL), _rows(tm, A_WIDTH)]
    if emit_vn:
        out_shape.append(jax.ShapeDtypeStruct((m, A_WIDTH), F32))
        out_specs.append(_rows(tm, A_WIDTH))
    return pl.pallas_call(
        functools.partial(_branch_a_kernel, tm=tm),
        grid=(m // tm,),
        in_specs=[_rows(tm, D_MODEL), _layer_block(l, (1, D_MODEL)), _w_in_block(l, 3 * A_WIDTH, R_A),
                  _layer_block(l, (1, A_WIDTH)), _layer_block(l, ws.shape[1:]),
                  _layer_block(l, bias_t.shape[1:])],
        out_specs=out_specs,
        out_shape=out_shape,
        compiler_params=_params(("parallel",)),
        name="branch_a",
    )(x, p['norm'], p['w_in'], p['a_v_gain'], ws, bias_t)


def _branch_c_kernel(xn_ref, w_ref, cw_ref, state_ref, yc_ref, new_ref, pad_ref, *, seg, carry, cb):
    tm = xn_ref.shape[0]
    xn = xn_ref[...]
    stride = seg + 8
    if carry:
        first = pl.program_id(0) % carry == 0

        @pl.when(first)
        def _():
            pad_ref[6:8, :] = state_ref[0]
    for j in range(C_WIDTH // cb):
        cs = slice(j * cb, (j + 1) * cb)
        h, b, c, z = [_dot(xn, w_ref[:, k * C_WIDTH + j * cb:k * C_WIDTH + (j + 1) * cb])
                      for k in range(4)]
        hc = c * h
        gate = b * _silu(z)
        w0, w1, w2 = cw_ref[0:1, cs], cw_ref[1:2, cs], cw_ref[2:3, cs]
        for s in range(tm // seg):
            base = s * stride
            rows = slice(s * seg, (s + 1) * seg)
            if not carry:
                pad_ref[base + 6:base + 8, cs] = state_ref[s, :, cs]
            pad_ref[base + 8:base + 8 + seg, cs] = hc[rows]
            y = (w2 * hc[rows]
                 + w1 * pad_ref[base + 7:base + 7 + seg, cs]
                 + w0 * pad_ref[base + 6:base + 6 + seg, cs])
            yc_ref[rows, cs] = (gate[rows] * y).astype(yc_ref.dtype)
            new_ref[s, :, cs] = hc[(s + 1) * seg - 2:(s + 1) * seg]
        if carry:
            pad_ref[6:8, cs] = hc[tm - 2:tm]


def _branch_c(xn, p, l, state, state_layer, tm, seg, tiles_per_stream, cb):
    m = xn.shape[0]
    n_streams = state.shape[1]
    nseg = tm // seg
    blk = (nseg, C_CONV - 1, C_WIDTH)
    if tiles_per_stream:
        st_in = pl.BlockSpec((None,) + blk, lambda i: (state_layer, i // tiles_per_stream, 0, 0))
        st_out = pl.BlockSpec(blk, lambda i: (i // tiles_per_stream, 0, 0))
    else:
        st_in = pl.BlockSpec((None,) + blk, lambda i: (state_layer, i, 0, 0))
        st_out = pl.BlockSpec(blk, lambda i: (i, 0, 0))
    return pl.pallas_call(
        functools.partial(_branch_c_kernel, seg=seg, carry=tiles_per_stream, cb=cb),
        grid=(m // tm,),
        in_specs=[_rows(tm, D_MODEL), _w_in_block(l, 4 * C_WIDTH, R_C),
                  _layer_block(l, (C_CONV, C_WIDTH)), st_in],
        out_specs=[_rows(tm, C_WIDTH), st_out],
        out_shape=[jax.ShapeDtypeStruct((m, C_WIDTH), BF16),
                   jax.ShapeDtypeStruct((n_streams, C_CONV - 1, C_WIDTH), F32)],
        scratch_shapes=[pltpu.VMEM((nseg * (seg + 8), C_WIDTH), F32)],
        compiler_params=_params(("arbitrary",)),
        name="branch_c",
    )(xn, p['w_in'], p['conv_w'], state)


def _rope_padded(x, cos_ref, sin_lo_ref, sin_hi_ref):
    return (x * cos_ref[...]
            + pltpu.roll(x, LANES - B_ROPE // 2, 1) * sin_lo_ref[...]
            + pltpu.roll(x, B_ROPE // 2, 1) * sin_hi_ref[...])


def _expand_kv(ckv_n, kpe_pad, wkvb_ref, kng_ref, k_ref, v_ref):
    kv = _dot(ckv_n.astype(BF16), wkvb_ref[...])
    v_ref[...] = kv[:, B_HEADS * B_NOPE:].astype(v_ref.dtype)
    kpe_b = kpe_pad.astype(k_ref.dtype)
    kn = [_rms(kv[:, hd * B_NOPE:(hd + 1) * B_NOPE], kng_ref[...]) for hd in range(B_HEADS)]
    for hd in range(B_HEADS):
        k_ref[:, hd * HEAD_PAD:hd * HEAD_PAD + LANES] = kn[hd].astype(k_ref.dtype)
        k_ref[:, hd * HEAD_PAD + LANES:(hd + 1) * HEAD_PAD] = kpe_b


def _branch_b_kernel(xn_ref, wcqkv_ref, wkpe_ref, wbz_ref, wqb_ref, wkvb_ref, qg_ref, kvg_ref, krg_ref,
                     qng_ref, qrg_ref, kng_ref, cos_ref, sin_lo_ref, sin_hi_ref, *rest, n_acc, emit_kv):
    q_ref, zb_ref, ckv_ref, kpe_ref = rest[n_acc:n_acc + 4]
    xn = xn_ref[...]
    half = B_WIDTH // 2
    cq = _rms(_dot(xn, wcqkv_ref[:, 0:B_Q_RANK]), qg_ref[...])
    zb_ref[:, 0:half] = _silu(_dot(xn, wbz_ref[:, 0:half]))
    q = _dot(cq.astype(BF16), wqb_ref[...])
    heads = range(B_HEADS)
    qa = [_rms(q[:, hd * HEAD_PAD:hd * HEAD_PAD + LANES], qng_ref[...]) for hd in heads]
    qb = [_rms(q[:, hd * HEAD_PAD + LANES:(hd + 1) * HEAD_PAD], qrg_ref[...], B_ROPE) for hd in heads]
    qb = [_rope_padded(qb[hd], cos_ref, sin_lo_ref, sin_hi_ref) for hd in heads]
    for hd in heads:
        q_ref[:, hd * HEAD_PAD:hd * HEAD_PAD + LANES] = qa[hd].astype(q_ref.dtype)
        q_ref[:, hd * HEAD_PAD + LANES:(hd + 1) * HEAD_PAD] = qb[hd].astype(q_ref.dtype)
    ckv_n = _rms(_dot(xn, wcqkv_ref[:, B_Q_RANK:B_Q_RANK + B_KV_RANK]), kvg_ref[...])
    ckv_ref[...] = ckv_n
    kpe_pad = _rope_padded(_rms(_dot(xn, wkpe_ref[...]), krg_ref[...], B_ROPE),
                           cos_ref, sin_lo_ref, sin_hi_ref)
    kpe_ref[...] = kpe_pad[:, 0:B_ROPE]
    zb_ref[:, half:] = _silu(_dot(xn, wbz_ref[:, half:]))
    if emit_kv:
        _expand_kv(ckv_n, kpe_pad, wkvb_ref, kng_ref, *rest[n_acc + 4:])


def _branch_b(xn, p, l, tables, tm, table_tiles, latents, emit_kv):
    m = xn.shape[0]
    depth = p['w_in'].shape[0]
    tab = pl.BlockSpec((tm, LANES), lambda i: (i % table_tiles, 0))
    gains = [p[n] for n in ('b_q_gain', 'b_kv_gain', 'b_kr_gain', 'b_qn_gain', 'b_qr_gain', 'b_kn_gain')]
    acc = [] if latents is None else list(latents)
    n_in = 15
    kv_specs = [_rows(tm, QK_WIDTH), _rows(tm, B_WIDTH)] if emit_kv else []
    kv_shapes = [jax.ShapeDtypeStruct((m, QK_WIDTH), BF16),
                 jax.ShapeDtypeStruct((m, B_WIDTH), BF16)] if emit_kv else []
    return pl.pallas_call(
        functools.partial(_branch_b_kernel, n_acc=len(acc), emit_kv=emit_kv),
        grid=(m // tm,),
        in_specs=[_rows(tm, D_MODEL),
                  _w_in_block(l, B_Q_RANK + B_KV_RANK, R_CQKV), _w_in_block(l, LANES, R_KPE),
                  _w_in_block(l, B_WIDTH, R_BZ),
                  _layer_block(l, (B_Q_RANK, QK_WIDTH)), _layer_block(l, (B_KV_RANK, 2 * B_WIDTH))]
                 + [_layer_block(l, g.shape[1:]) for g in gains] + [tab, tab, tab]
                 + [pl.BlockSpec(memory_space=pl.ANY)] * len(acc),
        out_specs=[_rows(tm, QK_WIDTH), _rows(tm, B_WIDTH),
                   pl.BlockSpec((None, tm, B_KV_RANK), lambda i: (l, i, 0)),
                   pl.BlockSpec((None, tm, B_ROPE), lambda i: (l, i, 0))] + kv_specs,
        out_shape=[jax.ShapeDtypeStruct((m, QK_WIDTH), BF16),
                   jax.ShapeDtypeStruct((m, B_WIDTH), F32),
                   jax.ShapeDtypeStruct((depth, m, B_KV_RANK), F32),
                   jax.ShapeDtypeStruct((depth, m, B_ROPE), F32)] + kv_shapes,
        input_output_aliases={n_in + j: 2 + j for j in range(len(acc))},
        compiler_params=_params(("parallel",)),
        name="branch_b",
    )(xn, p['w_in'], p['w_in'], p['w_in'], p['wqb'], p['wkvb'], *gains, *tables, *acc)


def _qk(q, k):
    return lax.dot_general(q, k, (((1,), (1,)), ((), ())), preferred_element_type=F32)


EXP2_SCALE = ATTN_SCALE * math.log2(math.e)


def _attn_prompt_kernel(q_ref, k_ref, v_ref, zb_ref, o_ref, vext_ref, *, tq, heads):
    s_len = q_ref.shape[0]
    rowc = lax.broadcasted_iota(jnp.int32, (tq, tq), 0) // CHUNK
    colc = lax.broadcasted_iota(jnp.int32, (tq, tq), 1) // CHUNK
    visible = colc <= rowc
    for hh in range(heads):
        vext_ref[hh, :, 0:B_VDIM] = v_ref[:, hh * B_VDIM:(hh + 1) * B_VDIM]
        vext_ref[hh, :, B_VDIM:] = jnp.ones((s_len, LANES), vext_ref.dtype)
    hs = range(heads)
    qs = [slice(hh * HEAD_PAD, (hh + 1) * HEAD_PAD) for hh in hs]
    vs = [slice(hh * B_VDIM, (hh + 1) * B_VDIM) for hh in hs]

    def scores(qi):
        r0 = qi * tq
        q_t = [q_ref[r0:r0 + tq, qs[hh]] for hh in hs]
        s_diag = [jnp.where(visible, _qk(q_t[hh], k_ref[r0:r0 + tq, qs[hh]]), -jnp.inf) for hh in hs]
        mx = [jnp.max(s_diag[hh], axis=-1, keepdims=True) for hh in hs]
        s_prev = None
        if qi:
            s_prev = [_qk(q_t[hh], k_ref[0:r0, qs[hh]]) for hh in hs]
            mx = [jnp.maximum(mx[hh], jnp.max(s_prev[hh], axis=-1, keepdims=True)) for hh in hs]
        return s_diag, s_prev, mx

    def weighted_values(qi, s_diag, s_prev, mx):
        r0 = qi * tq
        mc = [mx[hh] * EXP2_SCALE for hh in hs]
        p_diag = [jnp.exp2(s_diag[hh] * EXP2_SCALE - mc[hh]).astype(BF16) for hh in hs]
        acc = [_dot(p_diag[hh], vext_ref[hh, r0:r0 + tq, :]) for hh in hs]
        if qi:
            p_prev = [jnp.exp2(s_prev[hh] * EXP2_SCALE - mc[hh]).astype(BF16) for hh in hs]
            acc = [acc[hh] + _dot(p_prev[hh], vext_ref[hh, 0:r0, :]) for hh in hs]
        for hh in hs:
            o = acc[hh][:, 0:B_VDIM] / acc[hh][:, B_VDIM:]
            o_ref[r0:r0 + tq, vs[hh]] = (o * zb_ref[r0:r0 + tq, vs[hh]]).astype(o_ref.dtype)

    for qi in range(s_len // tq):
        weighted_values(qi, *scores(qi))


def _attn_prompt(q, k, v, zb, tq, heads):
    bn, s_len, _ = q.shape
    blk = lambda w: pl.BlockSpec((None, s_len, heads * w), lambda b, h: (b, 0, h))
    return pl.pallas_call(
        functools.partial(_attn_prompt_kernel, tq=tq, heads=heads),
        grid=(bn, B_HEADS // heads),
        in_specs=[blk(HEAD_PAD), blk(HEAD_PAD), blk(B_VDIM), blk(B_VDIM)],
        out_specs=blk(B_VDIM),
        out_shape=jax.ShapeDtypeStruct((bn, s_len, B_WIDTH), BF16),
        scratch_shapes=[pltpu.VMEM((heads, s_len, B_VDIM + LANES), BF16)],
        compiler_params=_params(("parallel", "parallel")),
        name="attn_prompt",
    )(q, k, v, zb)


SAMPLE_HEAD_GROUP = 4


def _attn_sample_kernel(ckv_ref, kpe_ref, ckvn_ref, kpen_ref, q_ref, zb_ref, wkvb_ref, kng_ref,
                        o_ref, kn_ref):
    past, t = ckv_ref.shape[0], ckvn_ref.shape[0]
    ckv = ckv_ref[...].astype(BF16)
    ckvn = ckvn_ref[...].astype(BF16)
    gw = SAMPLE_HEAD_GROUP * B_NOPE
    for g in range(B_HEADS // SAMPLE_HEAD_GROUP):
        kx = _dot(ckv, wkvb_ref[:, g * gw:(g + 1) * gw])
        kxn = _dot(ckvn, wkvb_ref[:, g * gw:(g + 1) * gw])
        for hd in range(SAMPLE_HEAD_GROUP):
            src = slice(hd * B_NOPE, (hd + 1) * B_NOPE)
            dst = slice(g * gw + hd * B_NOPE, g * gw + (hd + 1) * B_NOPE)
            kn_ref[0:past, dst] = _rms(kx[:, src], kng_ref[...]).astype(kn_ref.dtype)
            kn_ref[past:past + t, dst] = _rms(kxn[:, src], kng_ref[...]).astype(kn_ref.dtype)
    zero = jnp.zeros((t, B_NOPE), q_ref.dtype)
    qbd = jnp.concatenate(
        [jnp.concatenate([q_ref[:, hd * HEAD_PAD:hd * HEAD_PAD + B_NOPE] if g == hd else zero
                          for g in range(B_HEADS)], axis=1) for hd in range(B_HEADS)], axis=0)
    qpe = jnp.concatenate([q_ref[:, hd * HEAD_PAD + B_NOPE:hd * HEAD_PAD + B_QK]
                           for hd in range(B_HEADS)], axis=0)
    sp = (_qk(qbd, kn_ref[0:past, :]) + _qk(qpe, kpe_ref[...].astype(BF16))) * ATTN_SCALE
    sn = (_qk(qbd, kn_ref[past:past + t, :]) + _qk(qpe, kpen_ref[...].astype(BF16))) * ATTN_SCALE
    mx = jnp.maximum(jnp.max(sp, axis=-1, keepdims=True), jnp.max(sn, axis=-1, keepdims=True))
    pp = jnp.exp(sp - mx)
    pn = jnp.exp(sn - mx)
    inv = 1.0 / (jnp.sum(pp, axis=-1, keepdims=True) + jnp.sum(pn, axis=-1, keepdims=True))
    u = _dot((pp * inv).astype(BF16), ckv) + _dot((pn * inv).astype(BF16), ckvn)
    o_all = _dot(u.astype(BF16), wkvb_ref[:, B_HEADS * B_NOPE:])
    for hd in range(B_HEADS):
        vs = slice(hd * B_VDIM, (hd + 1) * B_VDIM)
        o_ref[:, vs] = (o_all[hd * t:(hd + 1) * t, vs] * zb_ref[:, vs]).astype(o_ref.dtype)


def _attn_sample(q, zb, cache_ckv, cache_kpe, latents, p, l, bn, t):
    past = cache_ckv.shape[1] // bn
    lrows = lambda n, w: pl.BlockSpec((None, n, w), lambda b: (l, b, 0))
    return pl.pallas_call(
        _attn_sample_kernel,
        grid=(bn,),
        in_specs=[lrows(past, B_KV_RANK), lrows(past, B_ROPE), lrows(t, B_KV_RANK), lrows(t, B_ROPE),
                  _rows(t, QK_WIDTH), _rows(t, B_WIDTH), _layer_block(l, (B_KV_RANK, 2 * B_WIDTH)),
                  _layer_block(l, (1, B_NOPE))],
        out_specs=_rows(t, B_WIDTH),
        out_shape=jax.ShapeDtypeStruct((bn * t, B_WIDTH), BF16),
        scratch_shapes=[pltpu.VMEM((past + t, B_HEADS * B_NOPE), BF16)],
        compiler_params=_params(("parallel",)),
        name="attn_sample",
    )(cache_ckv, cache_kpe, latents[0], latents[1], q, zb, p['wkvb'], p['b_kn_gain'])


def _merge_kernel(xn_ref, ya_ref, yb_ref, yc_ref, wg0_ref, wg1_ref, wg2_ref,
                  wpa_ref, wpb_ref, wpc_ref, h_ref):
    xn = xn_ref[...]
    h = (_sigmoid(_dot(xn, wg0_ref[...])) * _dot(ya_ref[...], wpa_ref[...])
         + _sigmoid(_dot(xn, wg1_ref[...])) * _dot(yb_ref[...], wpb_ref[...])
         + _sigmoid(_dot(xn, wg2_ref[...])) * _dot(yc_ref[...], wpc_ref[...]))
    h_ref[...] = h.astype(h_ref.dtype)


def _merge(xn, ya, yb, yc, p, l, tm, tn):
    m = xn.shape[0]
    nj = D_MODEL // tn
    row = lambda w: pl.BlockSpec((tm, w), lambda i, j: (i, 0))
    colw = lambda k, off: pl.BlockSpec((None, k, tn), lambda i, j: (l, 0, j + off))
    g0 = R_G // tn
    return pl.pallas_call(
        _merge_kernel,
        grid=(m // tm, nj),
        in_specs=[row(D_MODEL), row(A_WIDTH), row(B_WIDTH), row(C_WIDTH),
                  colw(D_MODEL, g0), colw(D_MODEL, g0 + nj), colw(D_MODEL, g0 + 2 * nj),
                  colw(A_WIDTH, 0), colw(B_WIDTH, 0), colw(C_WIDTH, 0)],
        out_specs=pl.BlockSpec((tm, tn), lambda i, j: (i, j)),
        out_shape=jax.ShapeDtypeStruct((m, D_MODEL), BF16),
        compiler_params=_params(("parallel", "parallel")),
        name="merge",
    )(xn, ya, yb, yc, p['w_in'], p['w_in'], p['w_in'], p['w_pa'], p['w_pb'], p['w_pc'])


def _out_kernel(x_ref, h_ref, w_ref, o_ref):
    o_ref[...] = x_ref[...] + _dot(h_ref[...], w_ref[...])


def _out_proj(x, h, p, l, tm):
    m = x.shape[0]
    return pl.pallas_call(
        _out_kernel,
        grid=(m // tm,),
        in_specs=[_rows(tm, D_MODEL), _rows(tm, D_MODEL), _layer_block(l, (D_MODEL, D_MODEL))],
        out_specs=_rows(tm, D_MODEL),
        out_shape=jax.ShapeDtypeStruct((m, D_MODEL), F32),
        compiler_params=_params(("parallel",)),
        name="out_proj",
    )(x, h, p['w_out'])


S_A = 0
S_CQKV = S_A + 3 * A_WIDTH
S_KPE = S_CQKV + B_Q_RANK + B_KV_RANK
S_BZ = S_KPE + B_ROPE
S_C = S_BZ + B_WIDTH
S_G = S_C + 4 * C_WIDTH
S_TOTAL = S_G + N_BRANCH * D_MODEL


RL_ROWS = 64
RL_PIECES = 8
RL_COLS = RL_ROWS * RL_PIECES
RL_SEGMENTS = ((R_C, S_C - R_C), (R_BZ, S_BZ - R_BZ), (R_A, S_A - R_A), (R_CQKV, S_CQKV - R_CQKV),
               (R_KPE, S_KPE - R_KPE), (R_G, S_G - R_G))
RL_KPE_BLOCK = R_KPE // RL_COLS


def _relayout_source_piece(j, k):
    delta = RL_SEGMENTS[0][1] // RL_ROWS
    for start, d in RL_SEGMENTS[1:]:
        delta = jnp.where(j >= start // RL_COLS, d // RL_ROWS, delta)
    return RL_PIECES * j + delta + k


def _relayout_kernel(*refs):
    o_ref = refs[-1]
    x = jnp.concatenate([r[...] for r in refs[:-1]], axis=0)
    row = lax.broadcasted_iota(jnp.int32, (RL_COLS, 1), 0)
    keep = jnp.logical_or(pl.program_id(1) != RL_KPE_BLOCK, row < B_ROPE)
    o_ref[...] = jnp.where(keep, x, 0.0).T.astype(o_ref.dtype)


def _relayout_w_in(w_in):
    depth = w_in.shape[0]
    assert w_in.shape[2] == S_TOTAL and R_TOTAL % RL_COLS == 0
    w_t = jnp.swapaxes(w_in, 1, 2)
    piece = lambda k: pl.BlockSpec((None, RL_ROWS, D_MODEL),
                                   lambda l, j: (l, _relayout_source_piece(j, k), 0))
    return pl.pallas_call(
        _relayout_kernel,
        grid=(depth, R_TOTAL // RL_COLS),
        in_specs=[piece(k) for k in range(RL_PIECES)],
        out_specs=pl.BlockSpec((None, D_MODEL, RL_COLS), lambda l, j: (l, 0, j)),
        out_shape=jax.ShapeDtypeStruct((depth, D_MODEL, R_TOTAL), BF16),
        compiler_params=_params(("parallel", "parallel")),
        name="relayout_w_in",
    )(*([w_t] * RL_PIECES))


def _rope_tables(pos):
    half = B_ROPE // 2
    inv = ROPE_BASE ** (-jnp.arange(half, dtype=F32) / half)
    ang = pos[:, None] * inv[None, :]
    c, s = jnp.cos(ang), jnp.sin(ang)
    z = jnp.zeros_like(c)
    return (jnp.concatenate([c, c, z, z], axis=1),
            jnp.concatenate([-s, z, z, z], axis=1),
            jnp.concatenate([z, s, z, z], axis=1))


def _prepare_params(norm_gain, w_in, a_v_gain, a_ws, a_bias, b_q_gain, b_w_qb, b_kv_gain, b_kr_gain,
                    b_w_kvb, b_qn_gain, b_qr_gain, b_kn_gain, c_conv_w, w_pa, w_pb, w_pc, w_out):
    depth = w_in.shape[0]
    w_r = _relayout_w_in(w_in)
    wqb = jnp.pad(b_w_qb.reshape(depth, B_Q_RANK, B_HEADS, B_QK),
                  ((0, 0), (0, 0), (0, 0), (0, HEAD_PAD - B_QK))).reshape(depth, B_Q_RANK, QK_WIDTH)
    wkv = b_w_kvb.reshape(depth, B_KV_RANK, B_HEADS, B_NOPE + B_VDIM)
    wkvb = jnp.concatenate([wkv[..., :B_NOPE].reshape(depth, B_KV_RANK, B_HEADS * B_NOPE),
                            wkv[..., B_NOPE:].reshape(depth, B_KV_RANK, B_WIDTH)], axis=2)
    row = lambda g: g[:, None, :]
    pad_rope = lambda g: jnp.pad(g, ((0, 0), (0, LANES - B_ROPE)))[:, None, :]
    return dict(
        norm=row(norm_gain), w_in=w_r, a_v_gain=row(a_v_gain), a_ws=a_ws, a_bias=a_bias,
        wqb=wqb.astype(BF16), wkvb=wkvb.astype(BF16),
        b_q_gain=row(b_q_gain), b_kv_gain=row(b_kv_gain), b_kr_gain=pad_rope(b_kr_gain),
        b_qn_gain=row(b_qn_gain), b_qr_gain=pad_rope(b_qr_gain), b_kn_gain=row(b_kn_gain),
        conv_w=c_conv_w,
        w_pa=w_pa.astype(BF16), w_pb=w_pb.astype(BF16), w_pc=w_pc.astype(BF16),
        w_out=w_out.astype(BF16))


def _finish_layer(x, xn, ya, yb, yc, p, l, tm_merge, tm_out):
    h = _merge(xn, ya, yb, yc, p, l, tm_merge, 256)
    return _out_proj(x, h, p, l, tm_out)


def _prompt_layer(x, p, l, tables, bn, s_len, latents, zero_state):
    tm = 1024
    xn, ya = _branch_a(x, p, l, p['a_ws'], jnp.swapaxes(p['a_bias'], 1, 2), tm, False)
    yc, conv_new = _branch_c(xn, p, l, zero_state, 0, tm, tm, s_len // tm, 512)
    tm_b = 256
    q, zb, ckv, kpe, k, v = _branch_b(xn, p, l, tables, tm_b, s_len // tm_b, latents, True)
    r3 = lambda a: a.reshape(bn, s_len, a.shape[-1])
    yb = _attn_prompt(r3(q), r3(k), r3(v), r3(zb), 256, 4).reshape(bn * s_len, B_WIDTH)
    out = _finish_layer(x, xn, ya, yb, yc, p, l, tm, tm)
    return out, (ckv, kpe), conv_new


def _sample_layer(x, p, l, tables, bn, t, ws_blk, bias_t, state_conv, cache_ckv, cache_kpe, latents):
    m = bn * t
    xn, ya, a_vn = _branch_a(x, p, l, ws_blk, bias_t, m, True)
    yc, conv_new = _branch_c(xn, p, l, state_conv, l, m, t, 0, C_WIDTH)
    q, zb, ckv, kpe = _branch_b(xn, p, l, tables, m, 1, latents, False)
    yb = _attn_sample(q, zb, cache_ckv, cache_kpe, (ckv, kpe), p, l, bn, t)
    out = _finish_layer(x, xn, ya, yb, yc, p, l, m, m)
    return out, (ckv, kpe), conv_new, a_vn


def kernel(x_prompt, x_sample, cache_ckv, cache_kpe, state_conv, norm_gain, w_in, a_v_gain, a_ws, a_bias,
           b_q_gain, b_w_qb, b_kv_gain, b_kr_gain, b_w_kvb, b_qn_gain, b_qr_gain, b_kn_gain, c_conv_w,
           w_pa, w_pb, w_pc, w_out):
    depth = w_in.shape[0]
    bn, s_len, _ = x_prompt.shape
    dbn, t, _ = x_sample.shape
    past = cache_ckv.shape[2]
    assert A_CHUNK % t == 0 and dbn * t == A_CHUNK and past % A_CHUNK == 0

    p = _prepare_params(norm_gain, w_in, a_v_gain, a_ws, a_bias, b_q_gain, b_w_qb, b_kv_gain, b_kr_gain,
                        b_w_kvb, b_qn_gain, b_qr_gain, b_kn_gain, c_conv_w, w_pa, w_pb, w_pc, w_out)
    tab_p = _rope_tables(jnp.arange(s_len, dtype=F32))
    tab_s = tuple(jnp.tile(tb, (dbn, 1)) for tb in _rope_tables(past + jnp.arange(t, dtype=F32)))

    hp = x_prompt.reshape(bn * s_len, D_MODEL)
    zero_state = jnp.zeros((1, bn, C_CONV - 1, C_WIDTH), F32)
    lat_p, conv_p = None, []
    for l in range(depth):
        hp, lat_p, cst = _prompt_layer(hp, p, l, tab_p, bn, s_len, lat_p, zero_state)
        conv_p.append(cst)

    m_s = dbn * t
    ws_blk = jnp.einsum('ab,lgts->lgatbs', jnp.eye(dbn, dtype=F32), a_ws[:, :, :t, :t]).reshape(
        depth, A_GROUPS, m_s, m_s)
    bias_s = jnp.swapaxes(jnp.tile(a_bias[:, :, :t], (1, 1, dbn)), 1, 2)
    ckv_c = cache_ckv.reshape(depth, dbn * past, B_KV_RANK)
    kpe_c = cache_kpe.reshape(depth, dbn * past, B_ROPE)
    hs = x_sample.reshape(m_s, D_MODEL)
    lat_s, conv_s, av_s = None, [], []
    for l in range(depth):
        hs, lat_s, cst, avn = _sample_layer(hs, p, l, tab_s, dbn, t, ws_blk, bias_s, state_conv,
                                            ckv_c, kpe_c, lat_s)
        conv_s.append(cst)
        av_s.append(avn.reshape(dbn, t, A_WIDTH))

    return (hp.reshape(bn, s_len, D_MODEL), hs.reshape(dbn, t, D_MODEL),
            lat_p[0].reshape(depth, bn, s_len, B_KV_RANK), lat_p[1].reshape(depth, bn, s_len, B_ROPE),
            jnp.stack(conv_p),
            lat_s[0].reshape(depth, dbn, t, B_KV_RANK), lat_s[1].reshape(depth, dbn, t, B_ROPE),
            jnp.stack(conv_s), jnp.stack(av_s))
```

```python
import functools
import math

import jax
import jax.numpy as jnp
from jax import lax
from jax.experimental import pallas as pl
from jax.experimental.pallas import tpu as pltpu

D_MODEL = 2048
CHUNK = 64
EPS = 1e-6
A_WIDTH = 1024
A_GROUPS = 4
A_GROUP_DIM = A_WIDTH // A_GROUPS
A_CHUNK = 128
B_HEADS = 16
B_Q_RANK = 512
B_KV_RANK = 512
B_NOPE = 128
B_ROPE = 64
B_VDIM = 128
B_QK = B_NOPE + B_ROPE
B_WIDTH = B_HEADS * B_VDIM
ATTN_SCALE = B_QK ** -0.5
ROPE_BASE = 10000.0
C_WIDTH = 1024
C_CONV = 3
N_BRANCH = 3

LANES = 128
HEAD_PAD = 2 * LANES
QK_WIDTH = B_HEADS * HEAD_PAD

R_C = 0
R_BZ = R_C + 4 * C_WIDTH
R_A = R_BZ + B_WIDTH
R_CQKV = R_A + 3 * A_WIDTH
R_KPE = R_CQKV + B_Q_RANK + B_KV_RANK
R_G = R_KPE + 4 * LANES
R_TOTAL = R_G + N_BRANCH * D_MODEL

VMEM_LIMIT = 56 * 1024 * 1024
BF16 = jnp.bfloat16
F32 = jnp.float32


def _dot(a, b):
    return jnp.dot(a, b, preferred_element_type=F32)


def _rms(x, gain, width=None):
    width = x.shape[-1] if width is None else width
    ms = jnp.sum(x * x, axis=-1, keepdims=True) * (1.0 / width)
    return x * lax.rsqrt(ms + EPS) * gain


def _silu(x):
    return x * (1.0 / (1.0 + jnp.exp(-x)))


def _sigmoid(x):
    return 1.0 / (1.0 + jnp.exp(-x))


def _params(sem):
    return pltpu.CompilerParams(dimension_semantics=sem, vmem_limit_bytes=VMEM_LIMIT)


def _layer_block(l, shape, blk=0):
    nd = len(shape)
    return pl.BlockSpec((None,) + tuple(shape), lambda *_: (l,) + (0,) * (nd - 1) + (blk,),
                        pipeline_mode=pl.Buffered(1))


def _w_in_block(l, width, offset):
    return _layer_block(l, (D_MODEL, width), offset // width)


def _rows(tm, width):
    return pl.BlockSpec((tm, width), lambda i: (i, 0))


def _branch_a_kernel(x_ref, ng_ref, w_ref, gain_ref, ws_ref, bias_ref, xn_ref, ya_ref, *vn_ref, tm):
    xn = _rms(x_ref[...], ng_ref[...]).astype(xn_ref.dtype)
    xn_ref[...] = xn
    row = lax.broadcasted_iota(jnp.int32, (A_CHUNK, A_CHUNK), 0)
    col = lax.broadcasted_iota(jnp.int32, (A_CHUNK, A_CHUNK), 1)
    for g in range(A_GROUPS):
        lo, hi = g * A_GROUP_DIM, (g + 1) * A_GROUP_DIM
        u = _dot(xn, w_ref[:, lo:hi])
        v = _dot(xn, w_ref[:, A_WIDTH + lo:A_WIDTH + hi])
        z = _dot(xn, w_ref[:, 2 * A_WIDTH + lo:2 * A_WIDTH + hi])
        vn = _rms(v, gain_ref[:, lo:hi])
        if vn_ref:
            vn_ref[0][:, lo:hi] = vn
        gate = u * _silu(z)
        ws = jnp.where(col <= row, ws_ref[g], 0.0).astype(BF16)
        bias = bias_ref[:, g:g + 1]
        vnb = vn.astype(BF16)
        for c in range(tm // A_CHUNK):
            r0, r1 = c * A_CHUNK, (c + 1) * A_CHUNK
            mixed = _dot(ws, vnb[r0:r1]) + bias
            ya_ref[r0:r1, lo:hi] = (gate[r0:r1] * mixed).astype(ya_ref.dtype)


def _branch_a(x, p, l, ws, bias_t, tm, emit_vn):
    m = x.shape[0]
    out_shape = [jax.ShapeDtypeStruct((m, D_MODEL), BF16), jax.ShapeDtypeStruct((m, A_WIDTH), BF16)]
    out_specs = [_rows(tm, D_MODEL), _rows(tm, A_WIDTH)]
    if emit_vn:
        out_shape.append(jax.ShapeDtypeStruct((m, A_WIDTH), F32))
        out_specs.append(_rows(tm, A_WIDTH))
    return pl.pallas_call(
        functools.partial(_branch_a_kernel, tm=tm),
        grid=(m // tm,),
        in_specs=[_rows(tm, D_MODEL), _layer_block(l, (1, D_MODEL)), _w_in_block(l, 3 * A_WIDTH, R_A),
                  _layer_block(l, (1, A_WIDTH)), _layer_block(l, ws.shape[1:]),
                  _layer_block(l, bias_t.shape[1:])],
        out_specs=out_specs,
        out_shape=out_shape,
        compiler_params=_params(("parallel",)),
        name="branch_a",
    )(x, p['norm'], p['w_in'], p['a_v_gain'], ws, bias_t)


def _branch_c_kernel(xn_ref, w_ref, cw_ref, state_ref, yc_ref, new_ref, pad_ref, *, seg, carry, cb):
    tm = xn_ref.shape[0]
    xn = xn_ref[...]
    stride = seg + 8
    if carry:
        first = pl.program_id(0) % carry == 0

        @pl.when(first)
        def _():
            pad_ref[6:8, :] = state_ref[0]
    for j in range(C_WIDTH // cb):
        cs = slice(j * cb, (j + 1) * cb)
        h, b, c, z = [_dot(xn, w_ref[:, k * C_WIDTH + j * cb:k * C_WIDTH + (j + 1) * cb])
                      for k in range(4)]
        hc = c * h
        gate = b * _silu(z)
        w0, w1, w2 = cw_ref[0:1, cs], cw_ref[1:2, cs], cw_ref[2:3, cs]
        for s in range(tm // seg):
            base = s * stride
            rows = slice(s * seg, (s + 1) * seg)
            if not carry:
                pad_ref[base + 6:base + 8, cs] = state_ref[s, :, cs]
            pad_ref[base + 8:base + 8 + seg, cs] = hc[rows]
            y = (w2 * hc[rows]
                 + w1 * pad_ref[base + 7:base + 7 + seg, cs]
                 + w0 * pad_ref[base + 6:base + 6 + seg, cs])
            yc_ref[rows, cs] = (gate[rows] * y).astype(yc_ref.dtype)
            new_ref[s, :, cs] = hc[(s + 1) * seg - 2:(s + 1) * seg]
        if carry:
            pad_ref[6:8, cs] = hc[tm - 2:tm]


def _branch_c(xn, p, l, state, state_layer, tm, seg, tiles_per_stream, cb):
    m = xn.shape[0]
    n_streams = state.shape[1]
    nseg = tm // seg
    blk = (nseg, C_CONV - 1, C_WIDTH)
    if tiles_per_stream:
        st_in = pl.BlockSpec((None,) + blk, lambda i: (state_layer, i // tiles_per_stream, 0, 0))
        st_out = pl.BlockSpec(blk, lambda i: (i // tiles_per_stream, 0, 0))
    else:
        st_in = pl.BlockSpec((None,) + blk, lambda i: (state_layer, i, 0, 0))
        st_out = pl.BlockSpec(blk, lambda i: (i, 0, 0))
    return pl.pallas_call(
        functools.partial(_branch_c_kernel, seg=seg, carry=tiles_per_stream, cb=cb),
        grid=(m // tm,),
        in_specs=[_rows(tm, D_MODEL), _w_in_block(l, 4 * C_WIDTH, R_C),
                  _layer_block(l, (C_CONV, C_WIDTH)), st_in],
        out_specs=[_rows(tm, C_WIDTH), st_out],
        out_shape=[jax.ShapeDtypeStruct((m, C_WIDTH), BF16),
                   jax.ShapeDtypeStruct((n_streams, C_CONV - 1, C_WIDTH), F32)],
        scratch_shapes=[pltpu.VMEM((nseg * (seg + 8), C_WIDTH), F32)],
        compiler_params=_params(("arbitrary",)),
        name="branch_c",
    )(xn, p['w_in'], p['conv_w'], state)


def _rope_padded(x, cos_ref, sin_lo_ref, sin_hi_ref):
    return (x * cos_ref[...]
            + pltpu.roll(x, LANES - B_ROPE // 2, 1) * sin_lo_ref[...]
            + pltpu.roll(x, B_ROPE // 2, 1) * sin_hi_ref[...])


def _expand_kv(ckv_n, kpe_pad, wkvb_ref, kng_ref, k_ref, v_ref):
    kv = _dot(ckv_n.astype(BF16), wkvb_ref[...])
    v_ref[...] = kv[:, B_HEADS * B_NOPE:].astype(v_ref.dtype)
    kpe_b = kpe_pad.astype(k_ref.dtype)
    kn = [_rms(kv[:, hd * B_NOPE:(hd + 1) * B_NOPE], kng_ref[...]) for hd in range(B_HEADS)]
    for hd in range(B_HEADS):
        k_ref[:, hd * HEAD_PAD:hd * HEAD_PAD + LANES] = kn[hd].astype(k_ref.dtype)
        k_ref[:, hd * HEAD_PAD + LANES:(hd + 1) * HEAD_PAD] = kpe_b


def _branch_b_kernel(xn_ref, wcqkv_ref, wkpe_ref, wbz_ref, wqb_ref, wkvb_ref, qg_ref, kvg_ref, krg_ref,
                     qng_ref, qrg_ref, kng_ref, cos_ref, sin_lo_ref, sin_hi_ref, *rest, n_acc, emit_kv):
    q_ref, zb_ref, ckv_ref, kpe_ref = rest[n_acc:n_acc + 4]
    xn = xn_ref[...]
    half = B_WIDTH // 2
    cq = _rms(_dot(xn, wcqkv_ref[:, 0:B_Q_RANK]), qg_ref[...])
    zb_ref[:, 0:half] = _silu(_dot(xn, wbz_ref[:, 0:half]))
    q = _dot(cq.astype(BF16), wqb_ref[...])
    heads = range(B_HEADS)
    qa = [_rms(q[:, hd * HEAD_PAD:hd * HEAD_PAD + LANES], qng_ref[...]) for hd in heads]
    qb = [_rms(q[:, hd * HEAD_PAD + LANES:(hd + 1) * HEAD_PAD], qrg_ref[...], B_ROPE) for hd in heads]
    qb = [_rope_padded(qb[hd], cos_ref, sin_lo_ref, sin_hi_ref) for hd in heads]
    for hd in heads:
        q_ref[:, hd * HEAD_PAD:hd * HEAD_PAD + LANES] = qa[hd].astype(q_ref.dtype)
        q_ref[:, hd * HEAD_PAD + LANES:(hd + 1) * HEAD_PAD] = qb[hd].astype(q_ref.dtype)
    ckv_n = _rms(_dot(xn, wcqkv_ref[:, B_Q_RANK:B_Q_RANK + B_KV_RANK]), kvg_ref[...])
    ckv_ref[...] = ckv_n
    kpe_pad = _rope_padded(_rms(_dot(xn, wkpe_ref[...]), krg_ref[...], B_ROPE),
                           cos_ref, sin_lo_ref, sin_hi_ref)
    kpe_ref[...] = kpe_pad[:, 0:B_ROPE]
    zb_ref[:, half:] = _silu(_dot(xn, wbz_ref[:, half:]))
    if emit_kv:
        _expand_kv(ckv_n, kpe_pad, wkvb_ref, kng_ref, *rest[n_acc + 4:])


def _branch_b(xn, p, l, tables, tm, table_tiles, latents, emit_kv):
    m = xn.shape[0]
    depth = p['w_in'].shape[0]
    tab = pl.BlockSpec((tm, LANES), lambda i: (i % table_tiles, 0))
    gains = [p[n] for n in ('b_q_gain', 'b_kv_gain', 'b_kr_gain', 'b_qn_gain', 'b_qr_gain', 'b_kn_gain')]
    acc = [] if latents is None else list(latents)
    n_in = 15
    kv_specs = [_rows(tm, QK_WIDTH), _rows(tm, B_WIDTH)] if emit_kv else []
    kv_shapes = [jax.ShapeDtypeStruct((m, QK_WIDTH), BF16),
                 jax.ShapeDtypeStruct((m, B_WIDTH), BF16)] if emit_kv else []
    return pl.pallas_call(
        functools.partial(_branch_b_kernel, n_acc=len(acc), emit_kv=emit_kv),
        grid=(m // tm,),
        in_specs=[_rows(tm, D_MODEL),
                  _w_in_block(l, B_Q_RANK + B_KV_RANK, R_CQKV), _w_in_block(l, LANES, R_KPE),
                  _w_in_block(l, B_WIDTH, R_BZ),
                  _layer_block(l, (B_Q_RANK, QK_WIDTH)), _layer_block(l, (B_KV_RANK, 2 * B_WIDTH))]
                 + [_layer_block(l, g.shape[1:]) for g in gains] + [tab, tab, tab]
                 + [pl.BlockSpec(memory_space=pl.ANY)] * len(acc),
        out_specs=[_rows(tm, QK_WIDTH), _rows(tm, B_WIDTH),
                   pl.BlockSpec((None, tm, B_KV_RANK), lambda i: (l, i, 0)),
                   pl.BlockSpec((None, tm, B_ROPE), lambda i: (l, i, 0))] + kv_specs,
        out_shape=[jax.ShapeDtypeStruct((m, QK_WIDTH), BF16),
                   jax.ShapeDtypeStruct((m, B_WIDTH), F32),
                   jax.ShapeDtypeStruct((depth, m, B_KV_RANK), F32),
                   jax.ShapeDtypeStruct((depth, m, B_ROPE), F32)] + kv_shapes,
        input_output_aliases={n_in + j: 2 + j for j in range(len(acc))},
        compiler_params=_params(("parallel",)),
        name="branch_b",
    )(xn, p['w_in'], p['w_in'], p['w_in'], p['wqb'], p['wkvb'], *gains, *tables, *acc)


def _qk(q, k):
    return lax.dot_general(q, k, (((1,), (1,)), ((), ())), preferred_element_type=F32)


EXP2_SCALE = ATTN_SCALE * math.log2(math.e)


def _attn_prompt_kernel(q_ref, k_ref, v_ref, zb_ref, o_ref, vext_ref, *, tq, heads):
    s_len = q_ref.shape[0]
    rowc = lax.broadcasted_iota(jnp.int32, (tq, tq), 0) // CHUNK
    colc = lax.broadcasted_iota(jnp.int32, (tq, tq), 1) // CHUNK
    visible = colc <= rowc
    for hh in range(heads):
        vext_ref[hh, :, 0:B_VDIM] = v_ref[:, hh * B_VDIM:(hh + 1) * B_VDIM]
        vext_ref[hh, :, B_VDIM:] = jnp.ones((s_len, LANES), vext_ref.dtype)
    hs = range(heads)
    qs = [slice(hh * HEAD_PAD, (hh + 1) * HEAD_PAD) for hh in hs]
    vs = [slice(hh * B_VDIM, (hh + 1) * B_VDIM) for hh in hs]

    def scores(qi):
        r0 = qi * tq
        q_t = [q_ref[r0:r0 + tq, qs[hh]] for hh in hs]
        s_diag = [jnp.where(visible, _qk(q_t[hh], k_ref[r0:r0 + tq, qs[hh]]), -jnp.inf) for hh in hs]
        mx = [jnp.max(s_diag[hh], axis=-1, keepdims=True) for hh in hs]
        s_prev = None
        if qi:
            s_prev = [_qk(q_t[hh], k_ref[0:r0, qs[hh]]) for hh in hs]
            mx = [jnp.maximum(mx[hh], jnp.max(s_prev[hh], axis=-1, keepdims=True)) for hh in hs]
        return s_diag, s_prev, mx

    def weighted_values(qi, s_diag, s_prev, mx):
        r0 = qi * tq
        mc = [mx[hh] * EXP2_SCALE for hh in hs]
        p_diag = [jnp.exp2(s_diag[hh] * EXP2_SCALE - mc[hh]).astype(BF16) for hh in hs]
        acc = [_dot(p_diag[hh], vext_ref[hh, r0:r0 + tq, :]) for hh in hs]
        if qi:
            p_prev = [jnp.exp2(s_prev[hh] * EXP2_SCALE - mc[hh]).astype(BF16) for hh in hs]
            acc = [acc[hh] + _dot(p_prev[hh], vext_ref[hh, 0:r0, :]) for hh in hs]
        for hh in hs:
            o = acc[hh][:, 0:B_VDIM] / acc[hh][:, B_VDIM:]
            o_ref[r0:r0 + tq, vs[hh]] = (o * zb_ref[r0:r0 + tq, vs[hh]]).astype(o_ref.dtype)

    for qi in range(s_len // tq):
        weighted_values(qi, *scores(qi))


def _attn_prompt(q, k, v, zb, tq, heads):
    bn, s_len, _ = q.shape
    blk = lambda w: pl.BlockSpec((None, s_len, heads * w), lambda b, h: (b, 0, h))
    return pl.pallas_call(
        functools.partial(_attn_prompt_kernel, tq=tq, heads=heads),
        grid=(bn, B_HEADS // heads),
        in_specs=[blk(HEAD_PAD), blk(HEAD_PAD), blk(B_VDIM), blk(B_VDIM)],
        out_specs=blk(B_VDIM),
        out_shape=jax.ShapeDtypeStruct((bn, s_len, B_WIDTH), BF16),
        scratch_shapes=[pltpu.VMEM((heads, s_len, B_VDIM + LANES), BF16)],
        compiler_params=_params(("parallel", "parallel")),
        name="attn_prompt",
    )(q, k, v, zb)


SAMPLE_HEAD_GROUP = 4


def _attn_sample_kernel(ckv_ref, kpe_ref, ckvn_ref, kpen_ref, q_ref, zb_ref, wkvb_ref, kng_ref,
                        o_ref, kn_ref):
    past, t = ckv_ref.shape[0], ckvn_ref.shape[0]
    ckv = ckv_ref[...].astype(BF16)
    ckvn = ckvn_ref[...].astype(BF16)
    gw = SAMPLE_HEAD_GROUP * B_NOPE
    for g in range(B_HEADS // SAMPLE_HEAD_GROUP):
        kx = _dot(ckv, wkvb_ref[:, g * gw:(g + 1) * gw])
        kxn = _dot(ckvn, wkvb_ref[:, g * gw:(g + 1) * gw])
        for hd in range(SAMPLE_HEAD_GROUP):
            src = slice(hd * B_NOPE, (hd + 1) * B_NOPE)
            dst = slice(g * gw + hd * B_NOPE, g * gw + (hd + 1) * B_NOPE)
            kn_ref[0:past, dst] = _rms(kx[:, src], kng_ref[...]).astype(kn_ref.dtype)
            kn_ref[past:past + t, dst] = _rms(kxn[:, src], kng_ref[...]).astype(kn_ref.dtype)
    zero = jnp.zeros((t, B_NOPE), q_ref.dtype)
    qbd = jnp.concatenate(
        [jnp.concatenate([q_ref[:, hd * HEAD_PAD:hd * HEAD_PAD + B_NOPE] if g == hd else zero
                          for g in range(B_HEADS)], axis=1) for hd in range(B_HEADS)], axis=0)
    qpe = jnp.concatenate([q_ref[:, hd * HEAD_PAD + B_NOPE:hd * HEAD_PAD + B_QK]
                           for hd in range(B_HEADS)], axis=0)
    sp = (_qk(qbd, kn_ref[0:past, :]) + _qk(qpe, kpe_ref[...].astype(BF16))) * ATTN_SCALE
    sn = (_qk(qbd, kn_ref[past:past + t, :]) + _qk(qpe, kpen_ref[...].astype(BF16))) * ATTN_SCALE
    mx = jnp.maximum(jnp.max(sp, axis=-1, keepdims=True), jnp.max(sn, axis=-1, keepdims=True))
    pp = jnp.exp(sp - mx)
    pn = jnp.exp(sn - mx)
    inv = 1.0 / (jnp.sum(pp, axis=-1, keepdims=True) + jnp.sum(pn, axis=-1, keepdims=True))
    u = _dot((pp * inv).astype(BF16), ckv) + _dot((pn * inv).astype(BF16), ckvn)
    o_all = _dot(u.astype(BF16), wkvb_ref[:, B_HEADS * B_NOPE:])
    for hd in range(B_HEADS):
        vs = slice(hd * B_VDIM, (hd + 1) * B_VDIM)
        o_ref[:, vs] = (o_all[hd * t:(hd + 1) * t, vs] * zb_ref[:, vs]).astype(o_ref.dtype)


def _attn_sample(q, zb, cache_ckv, cache_kpe, latents, p, l, bn, t):
    past = cache_ckv.shape[1] // bn
    lrows = lambda n, w: pl.BlockSpec((None, n, w), lambda b: (l, b, 0))
    return pl.pallas_call(
        _attn_sample_kernel,
        grid=(bn,),
        in_specs=[lrows(past, B_KV_RANK), lrows(past, B_ROPE), lrows(t, B_KV_RANK), lrows(t, B_ROPE),
                  _rows(t, QK_WIDTH), _rows(t, B_WIDTH), _layer_block(l, (B_KV_RANK, 2 * B_WIDTH)),
                  _layer_block(l, (1, B_NOPE))],
        out_specs=_rows(t, B_WIDTH),
        out_shape=jax.ShapeDtypeStruct((bn * t, B_WIDTH), BF16),
        scratch_shapes=[pltpu.VMEM((past + t, B_HEADS * B_NOPE), BF16)],
        compiler_params=_params(("parallel",)),
        name="attn_sample",
    )(cache_ckv, cache_kpe, latents[0], latents[1], q, zb, p['wkvb'], p['b_kn_gain'])


def _merge_kernel(xn_ref, ya_ref, yb_ref, yc_ref, wg0_ref, wg1_ref, wg2_ref,
                  wpa_ref, wpb_ref, wpc_ref, h_ref):
    xn = xn_ref[...]
    h = (_sigmoid(_dot(xn, wg0_ref[...])) * _dot(ya_ref[...], wpa_ref[...])
         + _sigmoid(_dot(xn, wg1_ref[...])) * _dot(yb_ref[...], wpb_ref[...])
         + _sigmoid(_dot(xn, wg2_ref[...])) * _dot(yc_ref[...], wpc_ref[...]))
    h_ref[...] = h.astype(h_ref.dtype)


def _merge(xn, ya, yb, yc, p, l, tm, tn):
    m = xn.shape[0]
    nj = D_MODEL // tn
    row = lambda w: pl.BlockSpec((tm, w), lambda i, j: (i, 0))
    colw = lambda k, off: pl.BlockSpec((None, k, tn), lambda i, j: (l, 0, j + off))
    g0 = R_G // tn
    return pl.pallas_call(
        _merge_kernel,
        grid=(m // tm, nj),
        in_specs=[pl.BlockSpec((tm, D_MODEL), lambda i, j: (i, 0), pipeline_mode=pl.Buffered(1)),
                  row(A_WIDTH), row(B_WIDTH), row(C_WIDTH),
                  colw(D_MODEL, g0), colw(D_MODEL, g0 + nj), colw(D_MODEL, g0 + 2 * nj),
                  colw(A_WIDTH, 0), colw(B_WIDTH, 0), colw(C_WIDTH, 0)],
        out_specs=pl.BlockSpec((tm, tn), lambda i, j: (i, j)),
        out_shape=jax.ShapeDtypeStruct((m, D_MODEL), BF16),
        compiler_params=_params(("parallel", "parallel")),
        name="merge",
    )(xn, ya, yb, yc, p['w_in'], p['w_in'], p['w_in'], p['w_pa'], p['w_pb'], p['w_pc'])


def _out_kernel(x_ref, h_ref, w_ref, o_ref):
    o_ref[...] = x_ref[...] + _dot(h_ref[...], w_ref[...])


def _out_proj(x, h, p, l, tm):
    m = x.shape[0]
    return pl.pallas_call(
        _out_kernel,
        grid=(m // tm,),
        in_specs=[_rows(tm, D_MODEL), _rows(tm, D_MODEL), _layer_block(l, (D_MODEL, D_MODEL))],
        out_specs=_rows(tm, D_MODEL),
        out_shape=jax.ShapeDtypeStruct((m, D_MODEL), F32),
        compiler_params=_params(("parallel",)),
        name="out_proj",
    )(x, h, p['w_out'])


S_A = 0
S_CQKV = S_A + 3 * A_WIDTH
S_KPE = S_CQKV + B_Q_RANK + B_KV_RANK
S_BZ = S_KPE + B_ROPE
S_C = S_BZ + B_WIDTH
S_G = S_C + 4 * C_WIDTH
S_TOTAL = S_G + N_BRANCH * D_MODEL


RL_ROWS = 64
RL_PIECES = 8
RL_COLS = RL_ROWS * RL_PIECES
RL_SEGMENTS = ((R_C, S_C - R_C), (R_BZ, S_BZ - R_BZ), (R_A, S_A - R_A), (R_CQKV, S_CQKV - R_CQKV),
               (R_KPE, S_KPE - R_KPE), (R_G, S_G - R_G))
RL_KPE_BLOCK = R_KPE // RL_COLS


def _relayout_source_piece(j, k):
    delta = RL_SEGMENTS[0][1] // RL_ROWS
    for start, d in RL_SEGMENTS[1:]:
        delta = jnp.where(j >= start // RL_COLS, d // RL_ROWS, delta)
    return RL_PIECES * j + delta + k


def _relayout_kernel(*refs):
    o_ref = refs[-1]
    x = jnp.concatenate([r[...] for r in refs[:-1]], axis=0)
    row = lax.broadcasted_iota(jnp.int32, (RL_COLS, 1), 0)
    keep = jnp.logical_or(pl.program_id(1) != RL_KPE_BLOCK, row < B_ROPE)
    o_ref[...] = jnp.where(keep, x, 0.0).T.astype(o_ref.dtype)


def _relayout_w_in(w_in):
    depth = w_in.shape[0]
    assert w_in.shape[2] == S_TOTAL and R_TOTAL % RL_COLS == 0
    w_t = jnp.swapaxes(w_in, 1, 2)
    piece = lambda k: pl.BlockSpec((None, RL_ROWS, D_MODEL),
                                   lambda l, j: (l, _relayout_source_piece(j, k), 0))
    return pl.pallas_call(
        _relayout_kernel,
        grid=(depth, R_TOTAL // RL_COLS),
        in_specs=[piece(k) for k in range(RL_PIECES)],
        out_specs=pl.BlockSpec((None, D_MODEL, RL_COLS), lambda l, j: (l, 0, j)),
        out_shape=jax.ShapeDtypeStruct((depth, D_MODEL, R_TOTAL), BF16),
        compiler_params=_params(("parallel", "parallel")),
        name="relayout_w_in",
    )(*([w_t] * RL_PIECES))


def _rope_tables(pos):
    half = B_ROPE // 2
    inv = ROPE_BASE ** (-jnp.arange(half, dtype=F32) / half)
    ang = pos[:, None] * inv[None, :]
    c, s = jnp.cos(ang), jnp.sin(ang)
    z = jnp.zeros_like(c)
    return (jnp.concatenate([c, c, z, z], axis=1),
            jnp.concatenate([-s, z, z, z], axis=1),
            jnp.concatenate([z, s, z, z], axis=1))


def _prepare_params(norm_gain, w_in, a_v_gain, a_ws, a_bias, b_q_gain, b_w_qb, b_kv_gain, b_kr_gain,
                    b_w_kvb, b_qn_gain, b_qr_gain, b_kn_gain, c_conv_w, w_pa, w_pb, w_pc, w_out):
    depth = w_in.shape[0]
    w_r = _relayout_w_in(w_in)
    wqb = jnp.pad(b_w_qb.reshape(depth, B_Q_RANK, B_HEADS, B_QK),
                  ((0, 0), (0, 0), (0, 0), (0, HEAD_PAD - B_QK))).reshape(depth, B_Q_RANK, QK_WIDTH)
    wkv = b_w_kvb.reshape(depth, B_KV_RANK, B_HEADS, B_NOPE + B_VDIM)
    wkvb = jnp.concatenate([wkv[..., :B_NOPE].reshape(depth, B_KV_RANK, B_HEADS * B_NOPE),
                            wkv[..., B_NOPE:].reshape(depth, B_KV_RANK, B_WIDTH)], axis=2)
    row = lambda g: g[:, None, :]
    pad_rope = lambda g: jnp.pad(g, ((0, 0), (0, LANES - B_ROPE)))[:, None, :]
    return dict(
        norm=row(norm_gain), w_in=w_r, a_v_gain=row(a_v_gain), a_ws=a_ws, a_bias=a_bias,
        wqb=wqb.astype(BF16), wkvb=wkvb.astype(BF16),
        b_q_gain=row(b_q_gain), b_kv_gain=row(b_kv_gain), b_kr_gain=pad_rope(b_kr_gain),
        b_qn_gain=row(b_qn_gain), b_qr_gain=pad_rope(b_qr_gain), b_kn_gain=row(b_kn_gain),
        conv_w=c_conv_w,
        w_pa=w_pa.astype(BF16), w_pb=w_pb.astype(BF16), w_pc=w_pc.astype(BF16),
        w_out=w_out.astype(BF16))


def _finish_layer(x, xn, ya, yb, yc, p, l, tm_merge, tm_out):
    h = _merge(xn, ya, yb, yc, p, l, tm_merge, 512)
    return _out_proj(x, h, p, l, tm_out)


def _prompt_layer(x, p, l, tables, bn, s_len, latents, zero_state):
    tm = 1024
    xn, ya = _branch_a(x, p, l, p['a_ws'], jnp.swapaxes(p['a_bias'], 1, 2), tm, False)
    yc, conv_new = _branch_c(xn, p, l, zero_state, 0, tm, tm, s_len // tm, 512)
    tm_b = 256
    q, zb, ckv, kpe, k, v = _branch_b(xn, p, l, tables, tm_b, s_len // tm_b, latents, True)
    r3 = lambda a: a.reshape(bn, s_len, a.shape[-1])
    yb = _attn_prompt(r3(q), r3(k), r3(v), r3(zb), 256, 4).reshape(bn * s_len, B_WIDTH)
    out = _finish_layer(x, xn, ya, yb, yc, p, l, tm, tm)
    return out, (ckv, kpe), conv_new


def _sample_layer(x, p, l, tables, bn, t, ws_blk, bias_t, state_conv, cache_ckv, cache_kpe, latents):
    m = bn * t
    xn, ya, a_vn = _branch_a(x, p, l, ws_blk, bias_t, m, True)
    yc, conv_new = _branch_c(xn, p, l, state_conv, l, m, t, 0, C_WIDTH)
    q, zb, ckv, kpe = _branch_b(xn, p, l, tables, m, 1, latents, False)
    yb = _attn_sample(q, zb, cache_ckv, cache_kpe, (ckv, kpe), p, l, bn, t)
    out = _finish_layer(x, xn, ya, yb, yc, p, l, m, m)
    return out, (ckv, kpe), conv_new, a_vn


def kernel(x_prompt, x_sample, cache_ckv, cache_kpe, state_conv, norm_gain, w_in, a_v_gain, a_ws, a_bias,
           b_q_gain, b_w_qb, b_kv_gain, b_kr_gain, b_w_kvb, b_qn_gain, b_qr_gain, b_kn_gain, c_conv_w,
           w_pa, w_pb, w_pc, w_out):
    depth = w_in.shape[0]
    bn, s_len, _ = x_prompt.shape
    dbn, t, _ = x_sample.shape
    past = cache_ckv.shape[2]
    assert A_CHUNK % t == 0 and dbn * t == A_CHUNK and past % A_CHUNK == 0

    p = _prepare_params(norm_gain, w_in, a_v_gain, a_ws, a_bias, b_q_gain, b_w_qb, b_kv_gain, b_kr_gain,
                        b_w_kvb, b_qn_gain, b_qr_gain, b_kn_gain, c_conv_w, w_pa, w_pb, w_pc, w_out)
    tab_p = _rope_tables(jnp.arange(s_len, dtype=F32))
    tab_s = tuple(jnp.tile(tb, (dbn, 1)) for tb in _rope_tables(past + jnp.arange(t, dtype=F32)))

    hp = x_prompt.reshape(bn * s_len, D_MODEL)
    zero_state = jnp.zeros((1, bn, C_CONV - 1, C_WIDTH), F32)
    lat_p, conv_p = None, []
    for l in range(depth):
        hp, lat_p, cst = _prompt_layer(hp, p, l, tab_p, bn, s_len, lat_p, zero_state)
        conv_p.append(cst)

    m_s = dbn * t
    ws_blk = jnp.einsum('ab,lgts->lgatbs', jnp.eye(dbn, dtype=F32), a_ws[:, :, :t, :t]).reshape(
        depth, A_GROUPS, m_s, m_s)
    bias_s = jnp.swapaxes(jnp.tile(a_bias[:, :, :t], (1, 1, dbn)), 1, 2)
    ckv_c = cache_ckv.reshape(depth, dbn * past, B_KV_RANK)
    kpe_c = cache_kpe.reshape(depth, dbn * past, B_ROPE)
    hs = x_sample.reshape(m_s, D_MODEL)
    lat_s, conv_s, av_s = None, [], []
    for l in range(depth):
        hs, lat_s, cst, avn = _sample_layer(hs, p, l, tab_s, dbn, t, ws_blk, bias_s, state_conv,
                                            ckv_c, kpe_c, lat_s)
        conv_s.append(cst)
        av_s.append(avn.reshape(dbn, t, A_WIDTH))

    return (hp.reshape(bn, s_len, D_MODEL), hs.reshape(dbn, t, D_MODEL),
            lat_p[0].reshape(depth, bn, s_len, B_KV_RANK), lat_p[1].reshape(depth, bn, s_len, B_ROPE),
            jnp.stack(conv_p),
            lat_s[0].reshape(depth, dbn, t, B_KV_RANK), lat_s[1].reshape(depth, dbn, t, B_ROPE),
            jnp.stack(conv_s), jnp.stack(av_s))
```
